```python
import jax, jax.numpy as jnp
from jax import lax
import numpy as np

D_MODEL = 1024
BATCH = 4
SEQ = 4096
DEPTH = 4

MLA_HEADS = 8
MLA_NOPE = 64
MLA_ROPE = 32
MLA_V = 64
MLA_Q_LORA = 256
MLA_KV_LORA = 128
ROPE_THETA = 10000.0
ATTN_BLOCK = 128
SG_HEADS = 4
SG_HEAD_DIM = 64
SG_CHUNK = 128
GLA_HEADS = 4
GLA_DK = 32
GLA_DV = 64
GLA_GATE_RANK = 16
GLA_GATE_TAU = 16.0
GLA_CHUNK = 64
D_FF = 2816
CONV_WIDTH = 3
EPS = 1e-6

D_MLA = MLA_HEADS * MLA_V
D_SG = SG_HEADS * SG_HEAD_DIM
D_GLA = GLA_HEADS * GLA_DV
D_MIX = D_MLA + D_SG + D_GLA
SPLIT_SIZES = (MLA_Q_LORA, MLA_KV_LORA, MLA_ROPE, D_SG, D_SG, GLA_HEADS * GLA_DK, GLA_HEADS * GLA_DK, D_GLA, GLA_GATE_RANK, D_GLA)
IN_COLS = 1712

kernel_name = 'hybrid_mla_sgu_gla_adaln_convffn'


def rms_norm(x, g):
    xf = x.astype(jnp.float32)
    y = xf * lax.rsqrt(jnp.mean(xf * xf, axis=-1, keepdims=True) + EPS)
    return (y * g.astype(jnp.float32)).astype(x.dtype)


def layer_norm(x, g, b):
    xf = x.astype(jnp.float32)
    mu = jnp.mean(xf, axis=-1, keepdims=True)
    var = jnp.mean(jnp.square(xf - mu), axis=-1, keepdims=True)
    return ((xf - mu) * lax.rsqrt(var + EPS) * g.astype(jnp.float32) + b.astype(jnp.float32)).astype(x.dtype)


def rope_tables(positions):
    inv_freq = ROPE_THETA ** (-jnp.arange(0, MLA_ROPE, 2, dtype=jnp.float32) / MLA_ROPE)
    ang = positions.astype(jnp.float32)[..., None] * inv_freq
    return jnp.cos(ang), jnp.sin(ang)


def apply_rope(t, cos, sin):
    t1, t2 = jnp.split(t, 2, axis=-1)
    return jnp.concatenate([t1 * cos - t2 * sin, t1 * sin + t2 * cos], axis=-1).astype(t.dtype)


def mla_attention(c_q, c_kv, k_rope_raw, q_norm_g, kv_norm_g, w_uq, w_ukv, cos, sin):
    B, S, _ = c_q.shape
    c_q = rms_norm(c_q, q_norm_g)
    c_kv = rms_norm(c_kv, kv_norm_g)
    q = (c_q @ w_uq).reshape(B, S, MLA_HEADS, MLA_NOPE + MLA_ROPE)
    q_nope = q[..., :MLA_NOPE]
    q_rope = apply_rope(q[..., MLA_NOPE:], cos[:, :, None, :], sin[:, :, None, :])
    kv = (c_kv @ w_ukv).reshape(B, S, MLA_HEADS, MLA_NOPE + MLA_V)
    k_nope = kv[..., :MLA_NOPE]
    v = kv[..., MLA_NOPE:]
    k_rope = apply_rope(k_rope_raw, cos, sin)
    scale = (MLA_NOPE + MLA_ROPE) ** -0.5
    nb = S // ATTN_BLOCK
    k_idx = jnp.arange(S)

    def to_blocks(t):
        return jnp.moveaxis(t.reshape(B, nb, ATTN_BLOCK, *t.shape[2:]), 1, 0)

    def attend(args):
        qn, qr, blk = args
        s = jnp.einsum('bqhd,bkhd->bhqk', qn, k_nope) + jnp.einsum('bqhd,bkd->bhqk', qr, k_rope)
        q_idx = blk * ATTN_BLOCK + jnp.arange(ATTN_BLOCK)
        causal = q_idx[:, None] >= k_idx[None, :]
        s = jnp.where(causal, s.astype(jnp.float32) * scale, -jnp.inf)
        p = jax.nn.softmax(s, axis=-1).astype(v.dtype)
        return jnp.einsum('bhqk,bkhd->bqhd', p, v)

    o = lax.map(attend, (to_blocks(q_nope), to_blocks(q_rope), jnp.arange(nb)))
    return jnp.moveaxis(o, 0, 1).reshape(B, S, D_MLA)


def spatial_gating(u, v, ln_g, ln_b, w_s, b_s):
    B, S, _ = u.shape
    u = jax.nn.gelu(u)
    v = layer_norm(jax.nn.gelu(v), ln_g, ln_b)
    nc = S // SG_CHUNK
    v = v.reshape(B, nc, SG_CHUNK, SG_HEADS, SG_HEAD_DIM)
    causal = jnp.tril(jnp.ones((SG_CHUNK, SG_CHUNK), dtype=bool))
    w = jnp.where(causal[None], w_s, 0).astype(v.dtype)
    mixed = jnp.einsum('hts,bnshd->bnthd', w, v) + b_s.T[None, None, :, :, None]
    return u * mixed.reshape(B, S, D_SG)


def gla(q, k, v, gate_lr, r, w_gate2, b_gate, norm_g):
    B, S, _ = q.shape
    f32 = jnp.float32
    log_a = jax.nn.log_sigmoid((gate_lr @ w_gate2 + b_gate).astype(f32)) / GLA_GATE_TAU
    nc = S // GLA_CHUNK

    def heads(t, d):
        return t.reshape(B, nc, GLA_CHUNK, GLA_HEADS, d).transpose(1, 0, 3, 2, 4)

    qc = heads(q.astype(f32) * GLA_DK ** -0.5, GLA_DK)
    kc = heads(k.astype(f32), GLA_DK)
    vc = heads(v.astype(f32), GLA_DV)
    gc = heads(log_a, GLA_DK)
    causal = jnp.tril(jnp.ones((GLA_CHUNK, GLA_CHUNK), dtype=bool))[None, None, :, :, None]

    def step(state, inp):
        qi, ki, vi, gi = inp
        b = jnp.cumsum(gi, axis=2)
        o_inter = jnp.einsum('bhcd,bhdv->bhcv', qi * jnp.exp(b), state)
        diff = jnp.where(causal, b[:, :, :, None, :] - b[:, :, None, :, :], -jnp.inf)
        a = jnp.einsum('bhid,bhjd,bhijd->bhij', qi, ki, jnp.exp(diff))
        o_intra = jnp.einsum('bhij,bhjv->bhiv', a, vi)
        b_last = b[:, :, -1:, :]
        state = jnp.exp(b_last[:, :, 0, :])[..., None] * state + jnp.einsum('bhjd,bhjv->bhdv', ki * jnp.exp(b_last - b), vi)
        return state, o_inter + o_intra

    s0 = jnp.zeros((B, GLA_HEADS, GLA_DK, GLA_DV), f32)
    _, o = lax.scan(step, s0, (qc, kc, vc, gc))
    o = o.transpose(1, 0, 3, 2, 4).reshape(B, S, GLA_HEADS, GLA_DV)
    o = rms_norm(o, norm_g).reshape(B, S, D_GLA)
    return (o * jax.nn.silu(r.astype(f32))).astype(q.dtype)


def conv_ffn(h, w_up, conv_w, conv_b, w_down):
    S = h.shape[1]
    z = h @ w_up
    zp = jnp.pad(z, ((0, 0), (CONV_WIDTH - 1, 0), (0, 0)))
    zc = conv_b + conv_w[CONV_WIDTH - 1] * zp[:, CONV_WIDTH - 1:CONV_WIDTH - 1 + S]
    for i in range(CONV_WIDTH - 1):
        zc = zc + conv_w[i] * zp[:, i:i + S]
    val, gate = jnp.split(zc, 2, axis=-1)
    return (jax.nn.silu(gate) * val) @ w_down


def setup_inputs(seed: int = 0) -> dict:
    key = jax.random.key(seed)
    ks = jax.random.split(key, 26)
    f32 = jnp.float32
    nrm = lambda k, shape, s: jax.random.normal(k, shape, f32) * s
    gain = lambda k, shape: 1.0 + 0.02 * jax.random.normal(k, shape, f32)
    return {
        'x': nrm(ks[0], (BATCH, SEQ, D_MODEL), 1.0),
        'c': nrm(ks[1], (BATCH, D_MODEL), 1.0),
        'positions': jnp.broadcast_to(jnp.arange(SEQ, dtype=jnp.int32), (BATCH, SEQ)),
        'mod_w': nrm(ks[2], (DEPTH, D_MODEL, 6 * D_MODEL), 0.5 * D_MODEL ** -0.5),
        'mod_b': nrm(ks[3], (DEPTH, 6 * D_MODEL), 0.01),
        'norm_mix_g': gain(ks[4], (DEPTH, D_MODEL)),
        'norm_ffn_g': gain(ks[5], (DEPTH, D_MODEL)),
        'w_in': nrm(ks[6], (DEPTH, D_MODEL, IN_COLS), D_MODEL ** -0.5),
        'mla_q_norm_g': gain(ks[7], (DEPTH, MLA_Q_LORA)),
        'mla_kv_norm_g': gain(ks[8], (DEPTH, MLA_KV_LORA)),
        'mla_w_uq': nrm(ks[9], (DEPTH, MLA_Q_LORA, MLA_HEADS * (MLA_NOPE + MLA_ROPE)), MLA_Q_LORA ** -0.5),
        'mla_w_ukv': nrm(ks[10], (DEPTH, MLA_KV_LORA, MLA_HEADS * (MLA_NOPE + MLA_V)), MLA_KV_LORA ** -0.5),
        'sg_ln_g': gain(ks[11], (DEPTH, D_SG)),
        'sg_ln_b': nrm(ks[12], (DEPTH, D_SG), 0.02),
        'sg_w_s': nrm(ks[13], (DEPTH, SG_HEADS, SG_CHUNK, SG_CHUNK), SG_CHUNK ** -0.5),
        'sg_b_s': gain(ks[14], (DEPTH, SG_HEADS, SG_CHUNK)),
        'gla_w_gate2': nrm(ks[15], (DEPTH, GLA_GATE_RANK, GLA_HEADS * GLA_DK), GLA_GATE_RANK ** -0.5),
        'gla_b_gate': nrm(ks[16], (DEPTH, GLA_HEADS * GLA_DK), 0.02),
        'gla_norm_g': gain(ks[17], (DEPTH, GLA_DV)),
        'w_out': nrm(ks[18], (DEPTH, D_MIX, D_MODEL), D_MIX ** -0.5),
        'ffn_w_up': nrm(ks[19], (DEPTH, D_MODEL, 2 * D_FF), D_MODEL ** -0.5),
        'ffn_conv_w': nrm(ks[20], (DEPTH, CONV_WIDTH, 2 * D_FF), CONV_WIDTH ** -0.5),
        'ffn_conv_b': nrm(ks[21], (DEPTH, 2 * D_FF), 0.02),
        'ffn_w_down': nrm(ks[22], (DEPTH, D_FF, D_MODEL), D_FF ** -0.5),
        'final_norm_g': gain(ks[23], (D_MODEL,)),
    }


def reference(x, c, positions, mod_w, mod_b, norm_mix_g, norm_ffn_g, w_in, mla_q_norm_g, mla_kv_norm_g, mla_w_uq, mla_w_ukv, sg_ln_g, sg_ln_b, sg_w_s, sg_b_s, gla_w_gate2, gla_b_gate, gla_norm_g, w_out, ffn_w_up, ffn_conv_w, ffn_conv_b, ffn_w_down, final_norm_g):
    cos, sin = rope_tables(positions)
    split_points = [int(p) for p in np.cumsum(SPLIT_SIZES)[:-1]]
    c_act = jax.nn.silu(c)
    for l in range(DEPTH):
        mod = (c_act @ mod_w[l] + mod_b[l])[:, None, :]
        sh1, sc1, g1, sh2, sc2, g2 = jnp.split(mod, 6, axis=-1)
        h = rms_norm(x, norm_mix_g[l]) * (1.0 + sc1) + sh1
        z = h @ w_in[l]
        cq, ckv, kr, su, sv, gq, gk, gv, glr, gr = jnp.split(z, split_points, axis=-1)
        o_a = mla_attention(cq, ckv, kr, mla_q_norm_g[l], mla_kv_norm_g[l], mla_w_uq[l], mla_w_ukv[l], cos, sin)
        o_b = spatial_gating(su, sv, sg_ln_g[l], sg_ln_b[l], sg_w_s[l], sg_b_s[l])
        o_c = gla(gq, gk, gv, glr, gr, gla_w_gate2[l], gla_b_gate[l], gla_norm_g[l])
        mix = jnp.concatenate([o_a, o_b, o_c], axis=-1) @ w_out[l]
        x = x + g1 * mix
        h2 = rms_norm(x, norm_ffn_g[l]) * (1.0 + sc2) + sh2
        x = x + g2 * conv_ffn(h2, ffn_w_up[l], ffn_conv_w[l], ffn_conv_b[l], ffn_w_down[l])
    return rms_norm(x, final_norm_g)
```

```python
import functools

import jax
import jax.numpy as jnp
from jax import lax
from jax.experimental import pallas as pl
from jax.experimental.pallas import tpu as pltpu

D_MODEL = 1024
MLA_HEADS = 8
MLA_NOPE = 64
MLA_ROPE = 32
MLA_V = 64
MLA_Q_LORA = 256
MLA_KV_LORA = 128
ROPE_THETA = 10000.0
SG_HEADS = 4
SG_HEAD_DIM = 64
SG_CHUNK = 128
GLA_HEADS = 4
GLA_DK = 32
GLA_DV = 64
GLA_GATE_RANK = 16
GLA_GATE_TAU = 16.0
GLA_CHUNK = 64
D_FF = 2816
CONV_WIDTH = 3
EPS = 1e-6

D_MLA = MLA_HEADS * MLA_V
D_SG = SG_HEADS * SG_HEAD_DIM
D_GLA = GLA_HEADS * GLA_DV
D_GQK = GLA_HEADS * GLA_DK

LANES = 128
HEAD_SLOT = LANES
ROPE_LO = MLA_NOPE
VMEM_LIMIT = 56 * 1024 * 1024

Z_CQ = 0
Z_CKV = Z_CQ + MLA_Q_LORA
Z_KR = Z_CKV + MLA_KV_LORA
Z_KRR = Z_KR + LANES
Z_SU = Z_KRR + LANES
Z_SV = Z_SU + D_SG
Z_GQ = Z_SV + D_SG
Z_GK = Z_GQ + D_GQK
Z_GV = Z_GK + D_GQK
Z_GLR = Z_GV + D_GLA
Z_GR = Z_GLR + LANES
Z_COLS = Z_GR + D_GLA

TM_PRE = 256
TQ_ATTN = 256
TG_GLA = 256
TM_POST = 256

F32 = jnp.float32
BF16 = jnp.bfloat16
NEG_BIG = -1e30


def _dot(a, b):
    return jnp.dot(a, b, preferred_element_type=F32)


def _dot_nt(a, b):
    return lax.dot_general(a, b, (((1,), (1,)), ((), ())), preferred_element_type=F32)


def _dot_tn(a, b):
    return lax.dot_general(a, b, (((0,), (0,)), ((), ())), preferred_element_type=F32)


def _sigmoid(x):
    return 1.0 / (1.0 + jnp.exp(-x))


def _gelu_tanh(x):
    return 0.5 * x * (1.0 + jnp.tanh(0.7978845608028654 * (x + 0.044715 * (x * x * x))))


def _log_sigmoid(x):
    return jnp.minimum(x, 0.0) - jnp.log1p(jnp.exp(-jnp.abs(x)))


def _rms(x, g):
    return x * lax.rsqrt(jnp.mean(x * x, axis=-1, keepdims=True) + EPS) * g


def _const_spec(shape):
    zeros = (0,) * len(shape)
    return pl.BlockSpec(shape, lambda *_: zeros, pipeline_mode=pl.Buffered(1))


def _mod_kernel(c_ref, w_ref, b_ref, o_ref):
    c = c_ref[...]
    ca = c * _sigmoid(c)
    o_ref[0] = jnp.dot(ca, w_ref[0], preferred_element_type=F32,
                       precision=lax.Precision.HIGHEST) + b_ref[0]


def _modulation(c, mod_w, mod_b):
    depth, d, n = mod_w.shape
    b = c.shape[0]
    tn = 1024
    return pl.pallas_call(
        _mod_kernel,
        grid=(depth, n // tn),
        in_specs=[
            pl.BlockSpec((b, d), lambda l, j: (0, 0)),
            pl.BlockSpec((1, d, tn), lambda l, j: (l, 0, j)),
            pl.BlockSpec((1, 1, tn), lambda l, j: (l, 0, j)),
        ],
        out_specs=pl.BlockSpec((1, b, tn), lambda l, j: (l, 0, j)),
        out_shape=jax.ShapeDtypeStruct((depth, b, n), F32),
        compiler_params=pltpu.CompilerParams(vmem_limit_bytes=VMEM_LIMIT),
        name="modulation",
    )(c, mod_w, mod_b.reshape(depth, 1, n))


def _rope_kernel(pos_ref, invf_ref, cos_ref, sin_ref):
    ang = pos_ref[0] * invf_ref[...]
    lane = lax.broadcasted_iota(jnp.int32, ang.shape, 1)
    rope = (lane >= ROPE_LO) & (lane < ROPE_LO + MLA_ROPE)
    cos_ref[0] = jnp.where(rope, jnp.cos(ang), jnp.where(lane < ROPE_LO, 1.0, 0.0))
    sin_ref[0] = jnp.where(rope, jnp.sin(ang), 0.0)


def _rope_tables(positions):
    b, s = positions.shape
    tm = 512
    half = MLA_ROPE // 2
    inv_freq = ROPE_THETA ** (-jnp.arange(0, MLA_ROPE, 2, dtype=F32) / MLA_ROPE)
    invf = jnp.zeros((1, LANES), F32)
    invf = invf.at[0, ROPE_LO:ROPE_LO + half].set(inv_freq)
    invf = invf.at[0, ROPE_LO + half:ROPE_LO + MLA_ROPE].set(inv_freq)
    pos = positions.astype(F32).reshape(b, s, 1)
    out = jax.ShapeDtypeStruct((b, s, LANES), F32)
    return pl.pallas_call(
        _rope_kernel,
        grid=(b, s // tm),
        in_specs=[pl.BlockSpec((1, tm, 1), lambda i, j: (i, j, 0)),
                  pl.BlockSpec((1, LANES), lambda i, j: (0, 0))],
        out_specs=[pl.BlockSpec((1, tm, LANES), lambda i, j: (i, j, 0))] * 2,
        out_shape=[out, out],
        name="rope_tables",
    )(pos, invf)


def _pre_kernel(x_ref, mod_ref, ng_ref, win_ref, qg_ref, kvg_ref, wuq_ref, wuqr_ref, wuk_ref,
                wuv_ref, cos_ref, sin_ref, lng_ref, lnb_ref, ws_ref, bs_ref, wg2_ref, bg_ref,
                q_out, k_out, v_out, sg_out, gq_out, gk_out, gla_out, gv_out, gr_out):
    tm = x_ref.shape[1]
    x = x_ref[0]
    sh1 = mod_ref[0, 0:1, :]
    sc1 = mod_ref[0, 1:2, :]
    h = _rms(x, ng_ref[...]) * (1.0 + sc1) + sh1
    z = _dot(h.astype(BF16), win_ref[...])

    cq = _rms(z[:, Z_CQ:Z_CQ + MLA_Q_LORA], qg_ref[...]).astype(BF16)
    ckv = _rms(z[:, Z_CKV:Z_CKV + MLA_KV_LORA], kvg_ref[...]).astype(BF16)
    cos = cos_ref[0]
    sin = sin_ref[0]
    qa = _dot(cq, wuq_ref[...])
    qb = _dot(cq, wuqr_ref[...])
    kn = _dot(ckv, wuk_ref[...])
    kr = z[:, Z_KR:Z_KR + LANES] * cos + z[:, Z_KRR:Z_KRR + LANES] * sin
    scale = float(MLA_NOPE + MLA_ROPE) ** -0.5
    for hd in range(MLA_HEADS):
        sl = slice(hd * HEAD_SLOT, (hd + 1) * HEAD_SLOT)
        q_out[0, :, sl] = ((qa[:, sl] * cos + qb[:, sl] * sin) * scale).astype(BF16)
        k_out[0, :, sl] = (kn[:, sl] + kr).astype(BF16)
    v_out[0] = _dot(ckv, wuv_ref[...]).astype(BF16)

    u = _gelu_tanh(z[:, Z_SU:Z_SU + D_SG])
    gv = _gelu_tanh(z[:, Z_SV:Z_SV + D_SG])
    mu = jnp.mean(gv, axis=-1, keepdims=True)
    dv = gv - mu
    var = jnp.mean(dv * dv, axis=-1, keepdims=True)
    vn = (dv * lax.rsqrt(var + EPS) * lng_ref[...] + lnb_ref[...]).astype(BF16)
    trow = lax.broadcasted_iota(jnp.int32, (SG_CHUNK, SG_CHUNK), 0)
    tcol = lax.broadcasted_iota(jnp.int32, (SG_CHUNK, SG_CHUNK), 1)
    wmix = [jnp.where(trow >= tcol, ws_ref[i], 0.0).astype(BF16) for i in range(SG_HEADS)]
    lane = lax.broadcasted_iota(jnp.int32, (SG_CHUNK, LANES), 1)
    lo_half = lane < SG_HEAD_DIM
    for c in range(tm // SG_CHUNK):
        rows = slice(c * SG_CHUNK, (c + 1) * SG_CHUNK)
        for p in range(D_SG // LANES):
            cols = slice(p * LANES, (p + 1) * LANES)
            vp = vn[rows, cols]
            mixed = jnp.where(lo_half, _dot(wmix[2 * p], vp), _dot(wmix[2 * p + 1], vp))
            sg_out[0, rows, cols] = (u[rows, cols] * (mixed + bs_ref[:, cols])).astype(BF16)

    gq_out[0] = z[:, Z_GQ:Z_GQ + D_GQK] * (float(GLA_DK) ** -0.5)
    gk_out[0] = z[:, Z_GK:Z_GK + D_GQK]
    gv_out[0] = z[:, Z_GV:Z_GV + D_GLA]
    gate = _dot(z[:, Z_GLR:Z_GLR + LANES].astype(BF16), wg2_ref[...]) + bg_ref[...]
    gla_out[0] = _log_sigmoid(gate) * (1.0 / GLA_GATE_TAU)
    gr = z[:, Z_GR:Z_GR + D_GLA]
    gr_out[0] = gr * _sigmoid(gr)


def _pre_call(x, mod_l, w, cos, sin):
    b, s, d = x.shape
    tm = TM_PRE
    tok = lambda n, dt: jax.ShapeDtypeStruct((b, s, n), dt)
    tspec = lambda n: pl.BlockSpec((1, tm, n), lambda i, j: (i, j, 0))
    consts = [w["ng1"], w["win"], w["qg"], w["kvg"], w["wuq"], w["wuqr"], w["wuk"], w["wuv"]]
    consts2 = [w["lng"], w["lnb"], w["ws"], w["bs"], w["wg2"], w["bg"]]
    in_specs = ([tspec(d), pl.BlockSpec((1, 6, d), lambda i, j: (i, 0, 0))]
                + [_const_spec(a.shape) for a in consts]
                + [tspec(LANES), tspec(LANES)]
                + [_const_spec(a.shape) for a in consts2])
    outs = [(MLA_HEADS * HEAD_SLOT, BF16), (MLA_HEADS * HEAD_SLOT, BF16), (D_MLA, BF16),
            (D_SG, BF16), (D_GQK, F32), (D_GQK, F32), (D_GQK, F32), (D_GLA, F32), (D_GLA, F32)]
    return pl.pallas_call(
        _pre_kernel,
        grid=(b, s // tm),
        in_specs=in_specs,
        out_specs=[tspec(n) for n, _ in outs],
        out_shape=[tok(n, dt) for n, dt in outs],
        compiler_params=pltpu.CompilerParams(
            dimension_semantics=("parallel", "parallel"), vmem_limit_bytes=VMEM_LIMIT),
        name="pre_mix",
    )(x, mod_l, *consts, cos, sin, *consts2)


def _attn_kernel(q_ref, k_ref, v_ref, o_ref):
    tq = q_ref.shape[1]
    tk = tq
    qi = pl.program_id(1)
    row = lax.broadcasted_iota(jnp.int32, (tq, tk), 0)
    col = lax.broadcasted_iota(jnp.int32, (tq, tk), 1)
    causal = row >= col
    lane = lax.broadcasted_iota(jnp.int32, (tq, LANES), 1)
    lo_half = lane < MLA_V

    for p in range(MLA_HEADS // 2):
        vcols = slice(p * LANES, (p + 1) * LANES)
        pair = []
        for hd in (2 * p, 2 * p + 1):
            hcols = slice(hd * HEAD_SLOT, (hd + 1) * HEAD_SLOT)
            q = q_ref[0, :, hcols]

            def step(j, carry, masked, q=q, hcols=hcols, vcols=vcols):
                m, l, acc = carry
                off = pl.multiple_of(j * tk, tk)
                k = k_ref[0, pl.ds(off, tk), hcols]
                v = v_ref[0, pl.ds(off, tk), vcols]
                s = _dot_nt(q, k)
                if masked:
                    s = jnp.where(causal, s, NEG_BIG)
                m_new = jnp.maximum(m, jnp.max(s, axis=-1, keepdims=True))
                alpha = jnp.exp(m - m_new)
                e = jnp.exp(s - m_new)
                l = alpha * l + jnp.sum(e, axis=-1, keepdims=True)
                acc = alpha * acc + _dot(e.astype(BF16), v)
                return m_new, l, acc

            init = (jnp.full((tq, 1), NEG_BIG, F32), jnp.zeros((tq, 1), F32),
                    jnp.zeros((tq, LANES), F32))
            carry = lax.fori_loop(0, qi, functools.partial(step, masked=False), init)
            m, l, acc = step(qi, carry, masked=True)
            pair.append(acc / l)
        o_ref[0, :, vcols] = jnp.where(lo_half, pair[0], pair[1]).astype(BF16)


def _attn_call(q, k, v):
    b, s, _ = q.shape
    tq = TQ_ATTN
    return pl.pallas_call(
        _attn_kernel,
        grid=(b, s // tq),
        in_specs=[pl.BlockSpec((1, tq, MLA_HEADS * HEAD_SLOT), lambda i, j: (i, j, 0)),
                  pl.BlockSpec((1, s, MLA_HEADS * HEAD_SLOT), lambda i, j: (i, 0, 0)),
                  pl.BlockSpec((1, s, D_MLA), lambda i, j: (i, 0, 0))],
        out_specs=pl.BlockSpec((1, tq, D_MLA), lambda i, j: (i, j, 0)),
        out_shape=jax.ShapeDtypeStruct((b, s, D_MLA), BF16),
        compiler_params=pltpu.CompilerParams(
            dimension_semantics=("parallel", "arbitrary"), vmem_limit_bytes=VMEM_LIMIT),
        name="mla_attention",
    )(q, k, v)


def _gla_kernel(q_ref, k_ref, la_ref, v_ref, r_ref, g_ref, o_ref, st_ref, kpad, bpad, vpad):
    tg = q_ref.shape[1]
    ch = GLA_CHUNK

    @pl.when(pl.program_id(1) == 0)
    def _():
        st_ref[...] = jnp.zeros_like(st_ref)
        kpad[0:ch, :] = jnp.zeros((ch, D_GQK), F32)
        bpad[0:ch, :] = jnp.zeros((ch, D_GQK), F32)
        vpad[0:ch, :] = jnp.zeros((ch, D_GLA), F32)

    q = q_ref[0]
    k = k_ref[0]
    v = v_ref[0]
    ti = lax.broadcasted_iota(jnp.int32, (tg, tg), 0)
    tj = lax.broadcasted_iota(jnp.int32, (tg, tg), 1)
    tri = jnp.where((tj <= ti) & ((ti // ch) == (tj // ch)), 1.0, 0.0)
    b = jnp.dot(tri, la_ref[0], preferred_element_type=F32, precision=lax.Precision.HIGHEST)
    kpad[ch:ch + tg, :] = k
    bpad[ch:ch + tg, :] = b
    vpad[ch:ch + tg, :] = v

    er = lax.broadcasted_iota(jnp.int32, (D_GQK, D_GLA), 0)
    ec = lax.broadcasted_iota(jnp.int32, (D_GQK, D_GLA), 1)
    head_sum = jnp.where((er // GLA_DK) == (ec // GLA_DV), 1.0, 0.0).astype(BF16)

    rowmod = lax.broadcasted_iota(jnp.int32, (tg, D_GQK), 0) % ch
    acc = jnp.zeros((tg, D_GLA), F32)
    for s in range(ch):
        ks = kpad[ch - s:ch - s + tg, :]
        bs = bpad[ch - s:ch - s + tg, :]
        vs = vpad[ch - s:ch - s + tg, :]
        p = jnp.where(rowmod >= s, q * ks * jnp.exp(b - bs), 0.0)
        acc = acc + _dot(p.astype(BF16), head_sum) * vs

    sr = lax.broadcasted_iota(jnp.int32, (D_GLA, D_GQK), 0)
    sc = lax.broadcasted_iota(jnp.int32, (D_GLA, D_GQK), 1)
    same_head = (sr // GLA_DV) == (sc // GLA_DK)
    gr = lax.broadcasted_iota(jnp.int32, (D_GLA, D_GLA), 0)
    gc = lax.broadcasted_iota(jnp.int32, (D_GLA, D_GLA), 1)
    head_mean = jnp.where((gr // GLA_DV) == (gc // GLA_DV), 1.0 / GLA_DV, 0.0).astype(BF16)
    for c in range(tg // ch):
        rows = slice(c * ch, (c + 1) * ch)
        bc = b[rows]
        b_last = bc[ch - 1:ch, :]
        st = st_ref[...]
        o_inter = _dot_nt((q[rows] * jnp.exp(bc)).astype(BF16), st.astype(BF16))
        kd = (k[rows] * jnp.exp(b_last - bc)).astype(BF16)
        upd = _dot_tn(v[rows].astype(BF16), kd)
        st_ref[...] = st * jnp.exp(b_last) + jnp.where(same_head, upd, 0.0)
        o = acc[rows] + o_inter
        o2 = o * o
        hi = o2.astype(BF16)
        lo = (o2 - hi.astype(F32)).astype(BF16)
        ms = _dot(hi, head_mean) + _dot(lo, head_mean)
        o_ref[0, rows, :] = (o * lax.rsqrt(ms + EPS) * g_ref[...] * r_ref[0, rows, :]).astype(BF16)


def _gla_call(gq, gk, gla, gv, gr, g256):
    b, s, _ = gq.shape
    tg = TG_GLA
    tspec = lambda n: pl.BlockSpec((1, tg, n), lambda i, j: (i, j, 0))
    return pl.pallas_call(
        _gla_kernel,
        grid=(b, s // tg),
        in_specs=[tspec(D_GQK), tspec(D_GQK), tspec(D_GQK), tspec(D_GLA), tspec(D_GLA),
                  _const_spec(g256.shape)],
        out_specs=tspec(D_GLA),
        out_shape=jax.ShapeDtypeStruct((b, s, D_GLA), BF16),
        scratch_shapes=[pltpu.VMEM((D_GLA, D_GQK), F32),
                        pltpu.VMEM((GLA_CHUNK + tg, D_GQK), F32),
                        pltpu.VMEM((GLA_CHUNK + tg, D_GQK), F32),
                        pltpu.VMEM((GLA_CHUNK + tg, D_GLA), F32)],
        compiler_params=pltpu.CompilerParams(
            dimension_semantics=("parallel", "arbitrary"), vmem_limit_bytes=VMEM_LIMIT),
        name="gla",
    )(gq, gk, gla, gv, gr, g256)


def _post_kernel(x_ref, oa_ref, ob_ref, oc_ref, mod_ref, wo_ref, ng_ref, wup_ref, cw_ref, cb_ref,
                 wdn_ref, fg_ref, o_ref, zbuf, act, *, final):
    tm = x_ref.shape[1]
    halo = 8
    g1 = mod_ref[0, 2:3, :]
    sh2 = mod_ref[0, 3:4, :]
    sc2 = mod_ref[0, 4:5, :]
    g2 = mod_ref[0, 5:6, :]
    mix = (_dot(oa_ref[0], wo_ref[0:D_MLA, :])
           + _dot(ob_ref[0], wo_ref[D_MLA:D_MLA + D_SG, :])
           + _dot(oc_ref[0], wo_ref[D_MLA + D_SG:, :]))
    x1 = x_ref[0] + g1 * mix
    h2 = _rms(x1, ng_ref[...]) * (1.0 + sc2) + sh2

    @pl.when(pl.program_id(1) == 0)
    def _():
        zbuf[0:halo, :] = jnp.zeros((halo, 2 * D_FF), F32)

    zbuf[halo:halo + tm, :] = _dot(h2.astype(BF16), wup_ref[...])
    cc = 256
    for j in range(D_FF // cc):
        parts = []
        for cols in (slice(j * cc, (j + 1) * cc), slice(D_FF + j * cc, D_FF + (j + 1) * cc)):
            zc = cb_ref[:, cols] + cw_ref[2:3, cols] * zbuf[halo:halo + tm, cols]
            zc = zc + cw_ref[0:1, cols] * zbuf[halo - 2:halo - 2 + tm, cols]
            zc = zc + cw_ref[1:2, cols] * zbuf[halo - 1:halo - 1 + tm, cols]
            parts.append(zc)
        val, gate = parts
        act[:, j * cc:(j + 1) * cc] = (gate * _sigmoid(gate) * val).astype(BF16)
    zbuf[halo - 2:halo, :] = zbuf[halo + tm - 2:halo + tm, :]
    x2 = x1 + g2 * _dot(act[...], wdn_ref[...])
    if final:
        x2 = _rms(x2, fg_ref[...])
    o_ref[0] = x2


def _post_call(x, oa, ob, oc, mod_l, w, final_g, final):
    b, s, d = x.shape
    tm = TM_POST
    tspec = lambda n: pl.BlockSpec((1, tm, n), lambda i, j: (i, j, 0))
    consts = [w["wo"], w["ng2"], w["wup"], w["cw"], w["cb"], w["wdn"], final_g]
    return pl.pallas_call(
        functools.partial(_post_kernel, final=final),
        grid=(b, s // tm),
        in_specs=[tspec(d), tspec(D_MLA), tspec(D_SG), tspec(D_GLA),
                  pl.BlockSpec((1, 6, d), lambda i, j: (i, 0, 0))]
                 + [_const_spec(a.shape) for a in consts],
        out_specs=tspec(d),
        out_shape=jax.ShapeDtypeStruct((b, s, d), F32),
        scratch_shapes=[pltpu.VMEM((8 + tm, 2 * D_FF), F32), pltpu.VMEM((tm, D_FF), BF16)],
        compiler_params=pltpu.CompilerParams(
            dimension_semantics=("parallel", "arbitrary"), vmem_limit_bytes=VMEM_LIMIT),
        name="post_ffn",
    )(x, oa, ob, oc, mod_l, *consts)


def _rot_half_cols(w):
    half = MLA_ROPE // 2
    return jnp.concatenate([-w[..., half:], w[..., :half]], axis=-1)


def _layer_weights(l, p):
    d = D_MODEL
    splits = (MLA_Q_LORA, MLA_KV_LORA, MLA_ROPE, D_SG, D_SG, D_GQK, D_GQK, D_GLA, GLA_GATE_RANK, D_GLA)
    offs = [0]
    for n in splits:
        offs.append(offs[-1] + n)
    w_in = p["w_in"][l]
    col = lambda i: w_in[:, offs[i]:offs[i + 1]]
    win = jnp.zeros((d, Z_COLS), F32)
    win = win.at[:, Z_CQ:Z_CQ + MLA_Q_LORA].set(col(0))
    win = win.at[:, Z_CKV:Z_CKV + MLA_KV_LORA].set(col(1))
    win = win.at[:, Z_KR + ROPE_LO:Z_KR + ROPE_LO + MLA_ROPE].set(col(2))
    win = win.at[:, Z_KRR + ROPE_LO:Z_KRR + ROPE_LO + MLA_ROPE].set(_rot_half_cols(col(2)))
    win = win.at[:, Z_SU:Z_SU + D_SG].set(col(3))
    win = win.at[:, Z_SV:Z_SV + D_SG].set(col(4))
    win = win.at[:, Z_GQ:Z_GQ + D_GQK].set(col(5))
    win = win.at[:, Z_GK:Z_GK + D_GQK].set(col(6))
    win = win.at[:, Z_GV:Z_GV + D_GLA].set(col(7))
    win = win.at[:, Z_GLR:Z_GLR + GLA_GATE_RANK].set(col(8))
    win = win.at[:, Z_GR:Z_GR + D_GLA].set(col(9))

    wq = p["mla_w_uq"][l].reshape(MLA_Q_LORA, MLA_HEADS, MLA_NOPE + MLA_ROPE)
    pad = HEAD_SLOT - MLA_NOPE - MLA_ROPE
    wuq = jnp.pad(wq, ((0, 0), (0, 0), (0, pad))).reshape(MLA_Q_LORA, MLA_HEADS * HEAD_SLOT)
    wqr = jnp.pad(_rot_half_cols(wq[..., MLA_NOPE:]), ((0, 0), (0, 0), (ROPE_LO, pad)))
    wuqr = wqr.reshape(MLA_Q_LORA, MLA_HEADS * HEAD_SLOT)
    wkv = p["mla_w_ukv"][l].reshape(MLA_KV_LORA, MLA_HEADS, MLA_NOPE + MLA_V)
    wuk = jnp.pad(wkv[..., :MLA_NOPE], ((0, 0), (0, 0), (0, HEAD_SLOT - MLA_NOPE)))
    wuk = wuk.reshape(MLA_KV_LORA, MLA_HEADS * HEAD_SLOT)
    wuv = wkv[..., MLA_NOPE:].reshape(MLA_KV_LORA, D_MLA)

    bs = jnp.repeat(p["sg_b_s"][l].T, SG_HEAD_DIM, axis=1)
    wg2 = jnp.zeros((LANES, D_GQK), F32).at[:GLA_GATE_RANK].set(p["gla_w_gate2"][l])
    row = lambda a: a.reshape(1, -1)
    return {
        "ng1": row(p["norm_mix_g"][l]), "win": win.astype(BF16),
        "qg": row(p["mla_q_norm_g"][l]), "kvg": row(p["mla_kv_norm_g"][l]),
        "wuq": wuq.astype(BF16), "wuqr": wuqr.astype(BF16), "wuk": wuk.astype(BF16),
        "wuv": wuv.astype(BF16),
        "lng": row(p["sg_ln_g"][l]), "lnb": row(p["sg_ln_b"][l]), "ws": p["sg_w_s"][l], "bs": bs,
        "wg2": wg2.astype(BF16), "bg": row(p["gla_b_gate"][l]),
        "gng": row(jnp.tile(p["gla_norm_g"][l], GLA_HEADS)),
        "wo": p["w_out"][l].astype(BF16), "ng2": row(p["norm_ffn_g"][l]),
        "wup": p["ffn_w_up"][l].astype(BF16), "cw": p["ffn_conv_w"][l],
        "cb": row(p["ffn_conv_b"][l]), "wdn": p["ffn_w_down"][l].astype(BF16),
    }


def kernel(x, c, positions, mod_w, mod_b, norm_mix_g, norm_ffn_g, w_in, mla_q_norm_g, mla_kv_norm_g, mla_w_uq, mla_w_ukv, sg_ln_g, sg_ln_b, sg_w_s, sg_b_s, gla_w_gate2, gla_b_gate, gla_norm_g, w_out, ffn_w_up, ffn_conv_w, ffn_conv_b, ffn_w_down, final_norm_g):
    p = dict(norm_mix_g=norm_mix_g, norm_ffn_g=norm_ffn_g, w_in=w_in, mla_q_norm_g=mla_q_norm_g,
             mla_kv_norm_g=mla_kv_norm_g, mla_w_uq=mla_w_uq, mla_w_ukv=mla_w_ukv, sg_ln_g=sg_ln_g,
             sg_ln_b=sg_ln_b, sg_w_s=sg_w_s, sg_b_s=sg_b_s, gla_w_gate2=gla_w_gate2,
             gla_b_gate=gla_b_gate, gla_norm_g=gla_norm_g, w_out=w_out, ffn_w_up=ffn_w_up,
             ffn_conv_w=ffn_conv_w, ffn_conv_b=ffn_conv_b, ffn_w_down=ffn_w_down)
    depth = mod_w.shape[0]
    b = x.shape[0]
    mod = _modulation(c, mod_w, mod_b).reshape(depth, b, 6, D_MODEL)
    cos, sin = _rope_tables(positions)
    fg = final_norm_g.reshape(1, -1)
    for l in range(depth):
        w = _layer_weights(l, p)
        q, k, v, o_sg, gq, gk, gla, gv, gr = _pre_call(x, mod[l], w, cos, sin)
        o_mla = _attn_call(q, k, v)
        o_gla = _gla_call(gq, gk, gla, gv, gr, w["gng"])
        x = _post_call(x, o_mla, o_sg, o_gla, mod[l], w, fg, final=(l == depth - 1))
    return x
```

```python
import functools

import jax
import jax.numpy as jnp
from jax import lax
from jax.experimental import pallas as pl
from jax.experimental.pallas import tpu as pltpu

D_MODEL = 1024
MLA_HEADS = 8
MLA_NOPE = 64
MLA_ROPE = 32
MLA_V = 64
MLA_Q_LORA = 256
MLA_KV_LORA = 128
ROPE_THETA = 10000.0
SG_HEADS = 4
SG_HEAD_DIM = 64
SG_CHUNK = 128
GLA_HEADS = 4
GLA_DK = 32
GLA_DV = 64
GLA_GATE_RANK = 16
GLA_GATE_TAU = 16.0
GLA_CHUNK = 64
D_FF = 2816
CONV_WIDTH = 3
EPS = 1e-6

D_MLA = MLA_HEADS * MLA_V
D_SG = SG_HEADS * SG_HEAD_DIM
D_GLA = GLA_HEADS * GLA_DV
D_GQK = GLA_HEADS * GLA_DK

LANES = 128
HEAD_SLOT = LANES
ROPE_LO = MLA_NOPE
VMEM_LIMIT = 56 * 1024 * 1024

Z_CQ = 0
Z_CKV = Z_CQ + MLA_Q_LORA
Z_KR = Z_CKV + MLA_KV_LORA
Z_KRR = Z_KR + LANES
Z_SU = Z_KRR + LANES
Z_SV = Z_SU + D_SG
Z_GQ = Z_SV + D_SG
Z_GK = Z_GQ + D_GQK
Z_GV = Z_GK + D_GQK
Z_GLR = Z_GV + D_GLA
Z_GR = Z_GLR + LANES
Z_COLS = Z_GR + D_GLA

TM_PRE = 512
TQ_ATTN = 256
TG_GLA = 256
ATTN_HEAD_GROUP = 8
TM_POST = 512

F32 = jnp.float32
BF16 = jnp.bfloat16
NEG_BIG = -1e30
LOG2_E = 1.4426950408889634


def _dot(a, b):
    return jnp.dot(a, b, preferred_element_type=F32)


def _dot_nt(a, b):
    return lax.dot_general(a, b, (((1,), (1,)), ((), ())), preferred_element_type=F32)


def _dot_tn(a, b):
    return lax.dot_general(a, b, (((0,), (0,)), ((), ())), preferred_element_type=F32)


def _sigmoid(x):
    return 1.0 / (1.0 + jnp.exp(-x))


def _gelu_tanh(x):
    return 0.5 * x * (1.0 + jnp.tanh(0.7978845608028654 * (x + 0.044715 * (x * x * x))))


def _log_sigmoid(x):
    return jnp.minimum(x, 0.0) - jnp.log1p(jnp.exp(-jnp.abs(x)))


def _rms(x, g):
    return x * lax.rsqrt(jnp.mean(x * x, axis=-1, keepdims=True) + EPS) * g


def _const_spec(shape):
    zeros = (0,) * len(shape)
    return pl.BlockSpec(shape, lambda *_: zeros, pipeline_mode=pl.Buffered(1))


def _mod_kernel(c_ref, w_ref, b_ref, o_ref):
    c = c_ref[...]
    ca = c * _sigmoid(c)
    o_ref[0] = jnp.dot(ca, w_ref[0], preferred_element_type=F32,
                       precision=lax.Precision.HIGHEST) + b_ref[0]


def _modulation(c, mod_w, mod_b):
    depth, d, n = mod_w.shape
    b = c.shape[0]
    tn = 1024
    return pl.pallas_call(
        _mod_kernel,
        grid=(depth, n // tn),
        in_specs=[
            pl.BlockSpec((b, d), lambda l, j: (0, 0)),
            pl.BlockSpec((1, d, tn), lambda l, j: (l, 0, j)),
            pl.BlockSpec((1, 1, tn), lambda l, j: (l, 0, j)),
        ],
        out_specs=pl.BlockSpec((1, b, tn), lambda l, j: (l, 0, j)),
        out_shape=jax.ShapeDtypeStruct((depth, b, n), F32),
        compiler_params=pltpu.CompilerParams(vmem_limit_bytes=VMEM_LIMIT),
        name="modulation",
    )(c, mod_w, mod_b.reshape(depth, 1, n))


def _rope_kernel(pos_ref, invf_ref, cos_ref, sin_ref):
    ang = pos_ref[0] * invf_ref[...]
    lane = lax.broadcasted_iota(jnp.int32, ang.shape, 1)
    rope = (lane >= ROPE_LO) & (lane < ROPE_LO + MLA_ROPE)
    cos_ref[0] = jnp.where(rope, jnp.cos(ang), jnp.where(lane < ROPE_LO, 1.0, 0.0))
    sin_ref[0] = jnp.where(rope, jnp.sin(ang), 0.0)


def _rope_tables(positions):
    b, s = positions.shape
    tm = 512
    half = MLA_ROPE // 2
    inv_freq = ROPE_THETA ** (-jnp.arange(0, MLA_ROPE, 2, dtype=F32) / MLA_ROPE)
    invf = jnp.zeros((1, LANES), F32)
    invf = invf.at[0, ROPE_LO:ROPE_LO + half].set(inv_freq)
    invf = invf.at[0, ROPE_LO + half:ROPE_LO + MLA_ROPE].set(inv_freq)
    pos = positions.astype(F32).reshape(b, s, 1)
    out = jax.ShapeDtypeStruct((b, s, LANES), F32)
    return pl.pallas_call(
        _rope_kernel,
        grid=(b, s // tm),
        in_specs=[pl.BlockSpec((1, tm, 1), lambda i, j: (i, j, 0)),
                  pl.BlockSpec((1, LANES), lambda i, j: (0, 0))],
        out_specs=[pl.BlockSpec((1, tm, LANES), lambda i, j: (i, j, 0))] * 2,
        out_shape=[out, out],
        name="rope_tables",
    )(pos, invf)


def _pre_kernel(x_ref, mod_ref, ng_ref, win_ref, qg_ref, kvg_ref, wuq_ref, wuqr_ref, wuk_ref,
                wuv_ref, cos_ref, sin_ref, lng_ref, lnb_ref, ws_ref, bs_ref, wg2_ref, bg_ref,
                q_out, k_out, v_out, sg_out, gq_out, gk_out, gla_out, gv_out, gr_out):
    tm = x_ref.shape[1]
    x = x_ref[0]
    sh1 = mod_ref[0, 0:1, :]
    sc1 = mod_ref[0, 1:2, :]
    h = _rms(x, ng_ref[...]) * (1.0 + sc1) + sh1
    z = _dot(h.astype(BF16), win_ref[...])

    cq = _rms(z[:, Z_CQ:Z_CQ + MLA_Q_LORA], qg_ref[...]).astype(BF16)
    ckv = _rms(z[:, Z_CKV:Z_CKV + MLA_KV_LORA], kvg_ref[...]).astype(BF16)
    cos = cos_ref[0]
    sin = sin_ref[0]
    qa = _dot(cq, wuq_ref[...])
    qb = _dot(cq, wuqr_ref[...])
    kn = _dot(ckv, wuk_ref[...])
    kr = z[:, Z_KR:Z_KR + LANES] * cos + z[:, Z_KRR:Z_KRR + LANES] * sin
    scale = float(MLA_NOPE + MLA_ROPE) ** -0.5 * LOG2_E
    for hd in range(MLA_HEADS):
        sl = slice(hd * HEAD_SLOT, (hd + 1) * HEAD_SLOT)
        q_out[0, :, sl] = ((qa[:, sl] * cos + qb[:, sl] * sin) * scale).astype(BF16)
        k_out[0, :, sl] = (kn[:, sl] + kr).astype(BF16)
    v_out[0] = _dot_nt(wuv_ref[...], ckv).astype(BF16)

    u = _gelu_tanh(z[:, Z_SU:Z_SU + D_SG])
    gv = _gelu_tanh(z[:, Z_SV:Z_SV + D_SG])
    mu = jnp.mean(gv, axis=-1, keepdims=True)
    dv = gv - mu
    var = jnp.mean(dv * dv, axis=-1, keepdims=True)
    vn = (dv * lax.rsqrt(var + EPS) * lng_ref[...] + lnb_ref[...]).astype(BF16)
    trow = lax.broadcasted_iota(jnp.int32, (SG_CHUNK, SG_CHUNK), 0)
    tcol = lax.broadcasted_iota(jnp.int32, (SG_CHUNK, SG_CHUNK), 1)
    wmix = [jnp.where(trow >= tcol, ws_ref[i], 0.0).astype(BF16) for i in range(SG_HEADS)]
    lane = lax.broadcasted_iota(jnp.int32, (SG_CHUNK, LANES), 1)
    lo_half = lane < SG_HEAD_DIM
    for c in range(tm // SG_CHUNK):
        rows = slice(c * SG_CHUNK, (c + 1) * SG_CHUNK)
        for p in range(D_SG // LANES):
            cols = slice(p * LANES, (p + 1) * LANES)
            vp = vn[rows, cols]
            mixed = jnp.where(lo_half, _dot(wmix[2 * p], vp), _dot(wmix[2 * p + 1], vp))
            sg_out[0, rows, cols] = (u[rows, cols] * (mixed + bs_ref[:, cols])).astype(BF16)

    gq_out[0] = z[:, Z_GQ:Z_GQ + D_GQK] * (float(GLA_DK) ** -0.5)
    gk_out[0] = z[:, Z_GK:Z_GK + D_GQK]
    gv_out[0] = z[:, Z_GV:Z_GV + D_GLA]
    gate = _dot(z[:, Z_GLR:Z_GLR + LANES].astype(BF16), wg2_ref[...]) + bg_ref[...]
    gla_out[0] = _log_sigmoid(gate) * (1.0 / GLA_GATE_TAU)
    gr = z[:, Z_GR:Z_GR + D_GLA]
    gr_out[0] = gr * _sigmoid(gr)


def _pre_call(x, mod_l, w, cos, sin):
    b, s, d = x.shape
    tm = TM_PRE
    tok = lambda n, dt: jax.ShapeDtypeStruct((b, s, n), dt)
    tspec = lambda n: pl.BlockSpec((1, tm, n), lambda i, j: (i, j, 0))
    consts = [w["ng1"], w["win"], w["qg"], w["kvg"], w["wuq"], w["wuqr"], w["wuk"], w["wuv"]]
    consts2 = [w["lng"], w["lnb"], w["ws"], w["bs"], w["wg2"], w["bg"]]
    in_specs = ([tspec(d), pl.BlockSpec((1, 6, d), lambda i, j: (i, 0, 0))]
                + [_const_spec(a.shape) for a in consts]
                + [tspec(LANES), tspec(LANES)]
                + [_const_spec(a.shape) for a in consts2])
    outs = [(MLA_HEADS * HEAD_SLOT, BF16), (MLA_HEADS * HEAD_SLOT, BF16), (D_MLA, BF16),
            (D_SG, BF16), (D_GQK, F32), (D_GQK, F32), (D_GQK, F32), (D_GLA, F32), (D_GLA, F32)]
    out_specs = [tspec(n) for n, _ in outs]
    out_shape = [tok(n, dt) for n, dt in outs]
    out_specs[2] = pl.BlockSpec((1, D_MLA, tm), lambda i, j: (i, 0, j))
    out_shape[2] = jax.ShapeDtypeStruct((b, D_MLA, s), BF16)
    return pl.pallas_call(
        _pre_kernel,
        grid=(b, s // tm),
        in_specs=in_specs,
        out_specs=out_specs,
        out_shape=out_shape,
        compiler_params=pltpu.CompilerParams(
            dimension_semantics=("parallel", "parallel"), vmem_limit_bytes=VMEM_LIMIT),
        name="pre_mix",
    )(x, mod_l, *consts, cos, sin, *consts2)


def _attn_kernel(q_ref, k_ref, vt_ref, o_ref, m_s, l_s, acc_s):
    tq = q_ref.shape[1]
    tk = tq
    qi = pl.program_id(1)
    krow = lax.broadcasted_iota(jnp.int32, (tk, tq), 0)
    qcol = lax.broadcasted_iota(jnp.int32, (tk, tq), 1)
    causal = krow <= qcol

    m_s[...] = jnp.full(m_s.shape, NEG_BIG, F32)
    l_s[...] = jnp.zeros(l_s.shape, F32)
    acc_s[...] = jnp.zeros(acc_s.shape, F32)

    def kv_step(j, masked):
        off = pl.multiple_of(j * tk, tk)
        for h0 in range(0, MLA_HEADS, ATTN_HEAD_GROUP):
            heads = range(h0, h0 + ATTN_HEAD_GROUP)
            sts = []
            for hd in heads:
                hcols = slice(hd * HEAD_SLOT, (hd + 1) * HEAD_SLOT)
                st = _dot_nt(k_ref[0, pl.ds(off, tk), hcols], q_ref[0, :, hcols])
                sts.append(jnp.where(causal, st, NEG_BIG) if masked else st)
            es, alphas = [], []
            for hd, st in zip(heads, sts):
                m_old = m_s[hd]
                m_new = jnp.maximum(m_old, jnp.max(st, axis=0, keepdims=True))
                alpha = jnp.exp2(m_old - m_new)
                e = jnp.exp2(st - m_new)
                l_s[hd] = alpha * l_s[hd] + jnp.sum(e, axis=0, keepdims=True)
                m_s[hd] = m_new
                es.append(e.astype(BF16))
                alphas.append(alpha)
            for hd, e, alpha in zip(heads, es, alphas):
                vrows = slice(hd * MLA_V, (hd + 1) * MLA_V)
                pv = _dot(vt_ref[0, vrows, pl.ds(off, tk)], e)
                acc_s[vrows, :] = alpha * acc_s[vrows, :] + pv

    def body(j, carry):
        kv_step(j, masked=False)
        return carry

    lax.fori_loop(0, qi, body, 0)
    kv_step(qi, masked=True)

    for p in range(MLA_HEADS // 2):
        inv = jnp.concatenate([jnp.broadcast_to(1.0 / l_s[2 * p], (MLA_V, tq)),
                               jnp.broadcast_to(1.0 / l_s[2 * p + 1], (MLA_V, tq))], axis=0)
        ot = acc_s[p * LANES:(p + 1) * LANES, :] * inv
        o_ref[0, :, p * LANES:(p + 1) * LANES] = ot.T.astype(BF16)


def _attn_call(q, k, vt):
    b, s, _ = q.shape
    tq = TQ_ATTN
    return pl.pallas_call(
        _attn_kernel,
        grid=(b, s // tq),
        in_specs=[pl.BlockSpec((1, tq, MLA_HEADS * HEAD_SLOT), lambda i, j: (i, j, 0)),
                  pl.BlockSpec((1, s, MLA_HEADS * HEAD_SLOT), lambda i, j: (i, 0, 0)),
                  pl.BlockSpec((1, D_MLA, s), lambda i, j: (i, 0, 0))],
        out_specs=pl.BlockSpec((1, tq, D_MLA), lambda i, j: (i, j, 0)),
        out_shape=jax.ShapeDtypeStruct((b, s, D_MLA), BF16),
        scratch_shapes=[pltpu.VMEM((MLA_HEADS, 1, tq), F32),
                        pltpu.VMEM((MLA_HEADS, 1, tq), F32),
                        pltpu.VMEM((D_MLA, tq), F32)],
        compiler_params=pltpu.CompilerParams(
            dimension_semantics=("parallel", "arbitrary"), vmem_limit_bytes=VMEM_LIMIT),
        name="mla_attention",
    )(q, k, vt)


def _gla_kernel(q_ref, k_ref, la_ref, v_ref, r_ref, g_ref, o_ref, st_ref, kpad, bpad, vpad):
    tg = q_ref.shape[1]
    ch = GLA_CHUNK

    @pl.when(pl.program_id(1) == 0)
    def _():
        st_ref[...] = jnp.zeros_like(st_ref)
        kpad[0:ch, :] = jnp.zeros((ch, D_GQK), F32)
        bpad[0:ch, :] = jnp.zeros((ch, D_GQK), F32)
        vpad[0:ch, :] = jnp.zeros((ch, D_GLA), F32)

    q = q_ref[0]
    k = k_ref[0]
    v = v_ref[0]
    ti = lax.broadcasted_iota(jnp.int32, (tg, tg), 0)
    tj = lax.broadcasted_iota(jnp.int32, (tg, tg), 1)
    tri = jnp.where((tj <= ti) & ((ti // ch) == (tj // ch)), 1.0, 0.0)
    b = jnp.dot(tri, la_ref[0], preferred_element_type=F32, precision=lax.Precision.HIGHEST)
    kpad[ch:ch + tg, :] = k
    bpad[ch:ch + tg, :] = b
    vpad[ch:ch + tg, :] = v

    er = lax.broadcasted_iota(jnp.int32, (D_GQK, D_GLA), 0)
    ec = lax.broadcasted_iota(jnp.int32, (D_GQK, D_GLA), 1)
    head_sum = jnp.where((er // GLA_DK) == (ec // GLA_DV), 1.0, 0.0).astype(BF16)

    rowmod = lax.broadcasted_iota(jnp.int32, (tg, D_GQK), 0) % ch
    acc = jnp.zeros((tg, D_GLA), F32)
    for s in range(ch):
        ks = kpad[ch - s:ch - s + tg, :]
        bs = bpad[ch - s:ch - s + tg, :]
        vs = vpad[ch - s:ch - s + tg, :]
        p = jnp.where(rowmod >= s, q * ks * jnp.exp(b - bs), 0.0)
        acc = acc + _dot(p.astype(BF16), head_sum) * vs

    sr = lax.broadcasted_iota(jnp.int32, (D_GLA, D_GQK), 0)
    sc = lax.broadcasted_iota(jnp.int32, (D_GLA, D_GQK), 1)
    same_head = (sr // GLA_DV) == (sc // GLA_DK)
    gr = lax.broadcasted_iota(jnp.int32, (D_GLA, D_GLA), 0)
    gc = lax.broadcasted_iota(jnp.int32, (D_GLA, D_GLA), 1)
    head_mean = jnp.where((gr // GLA_DV) == (gc // GLA_DV), 1.0 / GLA_DV, 0.0).astype(BF16)
    for c in range(tg // ch):
        rows = slice(c * ch, (c + 1) * ch)
        bc = b[rows]
        b_last = bc[ch - 1:ch, :]
        st = st_ref[...]
        o_inter = _dot_nt((q[rows] * jnp.exp(bc)).astype(BF16), st.astype(BF16))
        kd = (k[rows] * jnp.exp(b_last - bc)).astype(BF16)
        upd = _dot_tn(v[rows].astype(BF16), kd)
        st_ref[...] = st * jnp.exp(b_last) + jnp.where(same_head, upd, 0.0)
        o = acc[rows] + o_inter
        o2 = o * o
        hi = o2.astype(BF16)
        lo = (o2 - hi.astype(F32)).astype(BF16)
        ms = _dot(hi, head_mean) + _dot(lo, head_mean)
        o_ref[0, rows, :] = (o * lax.rsqrt(ms + EPS) * g_ref[...] * r_ref[0, rows, :]).astype(BF16)


def _gla_call(gq, gk, gla, gv, gr, g256):
    b, s, _ = gq.shape
    tg = TG_GLA
    tspec = lambda n: pl.BlockSpec((1, tg, n), lambda i, j: (i, j, 0))
    return pl.pallas_call(
        _gla_kernel,
        grid=(b, s // tg),
        in_specs=[tspec(D_GQK), tspec(D_GQK), tspec(D_GQK), tspec(D_GLA), tspec(D_GLA),
                  _const_spec(g256.shape)],
        out_specs=tspec(D_GLA),
        out_shape=jax.ShapeDtypeStruct((b, s, D_GLA), BF16),
        scratch_shapes=[pltpu.VMEM((D_GLA, D_GQK), F32),
                        pltpu.VMEM((GLA_CHUNK + tg, D_GQK), F32),
                        pltpu.VMEM((GLA_CHUNK + tg, D_GQK), F32),
                        pltpu.VMEM((GLA_CHUNK + tg, D_GLA), F32)],
        compiler_params=pltpu.CompilerParams(
            dimension_semantics=("parallel", "arbitrary"), vmem_limit_bytes=VMEM_LIMIT),
        name="gla",
    )(gq, gk, gla, gv, gr, g256)


def _post_kernel(x_ref, oa_ref, ob_ref, oc_ref, mod_ref, wo_ref, ng_ref, wup_ref, cw_ref, cb_ref,
                 wdn_ref, fg_ref, o_ref, zbuf, act, *, final):
    tm = x_ref.shape[1]
    halo = 8
    g1 = mod_ref[0, 2:3, :]
    sh2 = mod_ref[0, 3:4, :]
    sc2 = mod_ref[0, 4:5, :]
    g2 = mod_ref[0, 5:6, :]
    mix = (_dot(oa_ref[0], wo_ref[0:D_MLA, :])
           + _dot(ob_ref[0], wo_ref[D_MLA:D_MLA + D_SG, :])
           + _dot(oc_ref[0], wo_ref[D_MLA + D_SG:, :]))
    x1 = x_ref[0] + g1 * mix
    h2 = _rms(x1, ng_ref[...]) * (1.0 + sc2) + sh2

    @pl.when(pl.program_id(1) == 0)
    def _():
        zbuf[0:halo, :] = jnp.zeros((halo, 2 * D_FF), F32)

    zbuf[halo:halo + tm, :] = _dot(h2.astype(BF16), wup_ref[...])
    cc = 256
    for j in range(D_FF // cc):
        parts = []
        for cols in (slice(j * cc, (j + 1) * cc), slice(D_FF + j * cc, D_FF + (j + 1) * cc)):
            zc = cb_ref[:, cols] + cw_ref[2:3, cols] * zbuf[halo:halo + tm, cols]
            zc = zc + cw_ref[0:1, cols] * zbuf[halo - 2:halo - 2 + tm, cols]
            zc = zc + cw_ref[1:2, cols] * zbuf[halo - 1:halo - 1 + tm, cols]
            parts.append(zc)
        val, gate = parts
        act[:, j * cc:(j + 1) * cc] = (gate * _sigmoid(gate) * val).astype(BF16)
    zbuf[halo - 2:halo, :] = zbuf[halo + tm - 2:halo + tm, :]
    x2 = x1 + g2 * _dot(act[...], wdn_ref[...])
    if final:
        x2 = _rms(x2, fg_ref[...])
    o_ref[0] = x2


def _post_call(x, oa, ob, oc, mod_l, w, final_g, final):
    b, s, d = x.shape
    tm = TM_POST
    tspec = lambda n: pl.BlockSpec((1, tm, n), lambda i, j: (i, j, 0))
    consts = [w["wo"], w["ng2"], w["wup"], w["cw"], w["cb"], w["wdn"], final_g]
    return pl.pallas_call(
        functools.partial(_post_kernel, final=final),
        grid=(b, s // tm),
        in_specs=[tspec(d), tspec(D_MLA), tspec(D_SG), tspec(D_GLA),
                  pl.BlockSpec((1, 6, d), lambda i, j: (i, 0, 0))]
                 + [_const_spec(a.shape) for a in consts],
        out_specs=tspec(d),
        out_shape=jax.ShapeDtypeStruct((b, s, d), F32),
        scratch_shapes=[pltpu.VMEM((8 + tm, 2 * D_FF), F32), pltpu.VMEM((tm, D_FF), BF16)],
        compiler_params=pltpu.CompilerParams(
            dimension_semantics=("parallel", "arbitrary"), vmem_limit_bytes=VMEM_LIMIT),
        name="post_ffn",
    )(x, oa, ob, oc, mod_l, *consts)


def _rot_half_cols(w):
    half = MLA_ROPE // 2
    return jnp.concatenate([-w[..., half:], w[..., :half]], axis=-1)


def _layer_weights(l, p):
    d = D_MODEL
    splits = (MLA_Q_LORA, MLA_KV_LORA, MLA_ROPE, D_SG, D_SG, D_GQK, D_GQK, D_GLA, GLA_GATE_RANK, D_GLA)
    offs = [0]
    for n in splits:
        offs.append(offs[-1] + n)
    w_in = p["w_in"][l]
    col = lambda i: w_in[:, offs[i]:offs[i + 1]]
    win = jnp.zeros((d, Z_COLS), F32)
    win = win.at[:, Z_CQ:Z_CQ + MLA_Q_LORA].set(col(0))
    win = win.at[:, Z_CKV:Z_CKV + MLA_KV_LORA].set(col(1))
    win = win.at[:, Z_KR + ROPE_LO:Z_KR + ROPE_LO + MLA_ROPE].set(col(2))
    win = win.at[:, Z_KRR + ROPE_LO:Z_KRR + ROPE_LO + MLA_ROPE].set(_rot_half_cols(col(2)))
    win = win.at[:, Z_SU:Z_SU + D_SG].set(col(3))
    win = win.at[:, Z_SV:Z_SV + D_SG].set(col(4))
    win = win.at[:, Z_GQ:Z_GQ + D_GQK].set(col(5))
    win = win.at[:, Z_GK:Z_GK + D_GQK].set(col(6))
    win = win.at[:, Z_GV:Z_GV + D_GLA].set(col(7))
    win = win.at[:, Z_GLR:Z_GLR + GLA_GATE_RANK].set(col(8))
    win = win.at[:, Z_GR:Z_GR + D_GLA].set(col(9))

    wq = p["mla_w_uq"][l].reshape(MLA_Q_LORA, MLA_HEADS, MLA_NOPE + MLA_ROPE)
    pad = HEAD_SLOT - MLA_NOPE - MLA_ROPE
    wuq = jnp.pad(wq, ((0, 0), (0, 0), (0, pad))).reshape(MLA_Q_LORA, MLA_HEADS * HEAD_SLOT)
    wqr = jnp.pad(_rot_half_cols(wq[..., MLA_NOPE:]), ((0, 0), (0, 0), (ROPE_LO, pad)))
    wuqr = wqr.reshape(MLA_Q_LORA, MLA_HEADS * HEAD_SLOT)
    wkv = p["mla_w_ukv"][l].reshape(MLA_KV_LORA, MLA_HEADS, MLA_NOPE + MLA_V)
    wuk = jnp.pad(wkv[..., :MLA_NOPE], ((0, 0), (0, 0), (0, HEAD_SLOT - MLA_NOPE)))
    wuk = wuk.reshape(MLA_KV_LORA, MLA_HEADS * HEAD_SLOT)
    wuv = wkv[..., MLA_NOPE:].reshape(MLA_KV_LORA, D_MLA).T

    bs = jnp.repeat(p["sg_b_s"][l].T, SG_HEAD_DIM, axis=1)
    wg2 = jnp.zeros((LANES, D_GQK), F32).at[:GLA_GATE_RANK].set(p["gla_w_gate2"][l])
    row = lambda a: a.reshape(1, -1)
    return {
        "ng1": row(p["norm_mix_g"][l]), "win": win.astype(BF16),
        "qg": row(p["mla_q_norm_g"][l]), "kvg": row(p["mla_kv_norm_g"][l]),
        "wuq": wuq.astype(BF16), "wuqr": wuqr.astype(BF16), "wuk": wuk.astype(BF16),
        "wuv": wuv.astype(BF16),
        "lng": row(p["sg_ln_g"][l]), "lnb": row(p["sg_ln_b"][l]), "ws": p["sg_w_s"][l], "bs": bs,
        "wg2": wg2.astype(BF16), "bg": row(p["gla_b_gate"][l]),
        "gng": row(jnp.tile(p["gla_norm_g"][l], GLA_HEADS)),
        "wo": p["w_out"][l].astype(BF16), "ng2": row(p["norm_ffn_g"][l]),
        "wup": p["ffn_w_up"][l].astype(BF16), "cw": p["ffn_conv_w"][l],
        "cb": row(p["ffn_conv_b"][l]), "wdn": p["ffn_w_down"][l].astype(BF16),
    }


def kernel(x, c, positions, mod_w, mod_b, norm_mix_g, norm_ffn_g, w_in, mla_q_norm_g, mla_kv_norm_g, mla_w_uq, mla_w_ukv, sg_ln_g, sg_ln_b, sg_w_s, sg_b_s, gla_w_gate2, gla_b_gate, gla_norm_g, w_out, ffn_w_up, ffn_conv_w, ffn_conv_b, ffn_w_down, final_norm_g):
    p = dict(norm_mix_g=norm_mix_g, norm_ffn_g=norm_ffn_g, w_in=w_in, mla_q_norm_g=mla_q_norm_g,
             mla_kv_norm_g=mla_kv_norm_g, mla_w_uq=mla_w_uq, mla_w_ukv=mla_w_ukv, sg_ln_g=sg_ln_g,
             sg_ln_b=sg_ln_b, sg_w_s=sg_w_s, sg_b_s=sg_b_s, gla_w_gate2=gla_w_gate2,
             gla_b_gate=gla_b_gate, gla_norm_g=gla_norm_g, w_out=w_out, ffn_w_up=ffn_w_up,
             ffn_conv_w=ffn_conv_w, ffn_conv_b=ffn_conv_b, ffn_w_down=ffn_w_down)
    depth = mod_w.shape[0]
    b = x.shape[0]
    mod = _modulation(c, mod_w, mod_b).reshape(depth, b, 6, D_MODEL)
    cos, sin = _rope_tables(positions)
    fg = final_norm_g.reshape(1, -1)
    for l in range(depth):
        w = _layer_weights(l, p)
        q, k, v, o_sg, gq, gk, gla, gv, gr = _pre_call(x, mod[l], w, cos, sin)
        o_mla = _attn_call(q, k, v)
        o_gla = _gla_call(gq, gk, gla, gv, gr, w["gng"])
        x = _post_call(x, o_mla, o_sg, o_gla, mod[l], w, fg, final=(l == depth - 1))
    return x
```

```python
import functools

import jax
import jax.numpy as jnp
from jax import lax
from jax.experimental import pallas as pl
from jax.experimental.pallas import tpu as pltpu

D_MODEL = 1024
MLA_HEADS = 8
MLA_NOPE = 64
MLA_ROPE = 32
MLA_V = 64
MLA_Q_LORA = 256
MLA_KV_LORA = 128
ROPE_THETA = 10000.0
SG_HEADS = 4
SG_HEAD_DIM = 64
SG_CHUNK = 128
GLA_HEADS = 4
GLA_DK = 32
GLA_DV = 64
GLA_GATE_RANK = 16
GLA_GATE_TAU = 16.0
GLA_CHUNK = 64
GLA_SUB = 16
D_FF = 2816
CONV_WIDTH = 3
EPS = 1e-6

D_MLA = MLA_HEADS * MLA_V
D_SG = SG_HEADS * SG_HEAD_DIM
D_GLA = GLA_HEADS * GLA_DV
D_GQK = GLA_HEADS * GLA_DK

LANES = 128
HEAD_SLOT = LANES
ROPE_LO = MLA_NOPE
VT_ROWS = MLA_V + 16
VMEM_LIMIT = 56 * 1024 * 1024

Z_CQ = 0
Z_CKV = Z_CQ + MLA_Q_LORA
Z_KR = Z_CKV + MLA_KV_LORA
Z_KRR = Z_KR + LANES
Z_SU = Z_KRR + LANES
Z_SV = Z_SU + D_SG
Z_GQ = Z_SV + D_SG
Z_GK = Z_GQ + D_GQK
Z_GV = Z_GK + D_GQK
Z_GLR = Z_GV + D_GLA
Z_GR = Z_GLR + LANES
Z_COLS = Z_GR + D_GLA

TM_PRE = 512
TQ_ATTN = 512
TG_GLA = 512
ATTN_LOOKAHEAD = 2
TM_POST = 512

F32 = jnp.float32
BF16 = jnp.bfloat16
NEG_BIG = -1e30
LOG2_E = 1.4426950408889634


def _dot(a, b):
    return jnp.dot(a, b, preferred_element_type=F32)


def _dot_nt(a, b):
    return lax.dot_general(a, b, (((1,), (1,)), ((), ())), preferred_element_type=F32)


def _dot_tn(a, b):
    return lax.dot_general(a, b, (((0,), (0,)), ((), ())), preferred_element_type=F32)


def _sigmoid(x):
    return 1.0 / (1.0 + jnp.exp(-x))


def _gelu_tanh(x):
    return 0.5 * x * (1.0 + jnp.tanh(0.7978845608028654 * (x + 0.044715 * (x * x * x))))


def _log_sigmoid(x):
    return jnp.minimum(x, 0.0) - jnp.log1p(jnp.exp(-jnp.abs(x)))


def _rms(x, g):
    return x * lax.rsqrt(jnp.mean(x * x, axis=-1, keepdims=True) + EPS) * g


def _const_spec(shape):
    zeros = (0,) * len(shape)
    return pl.BlockSpec(shape, lambda *_: zeros, pipeline_mode=pl.Buffered(1))


def _layer_spec(shape, l):
    idx = (l,) + (0,) * (len(shape) - 1)
    return pl.BlockSpec((1,) + tuple(shape[1:]), lambda *_: idx, pipeline_mode=pl.Buffered(1))


def _mod_spec(d, l):
    return pl.BlockSpec((1, 1, 6, d), lambda i, j: (l, i, 0, 0))


def _mod_kernel(c_ref, w_ref, b_ref, o_ref):
    c = c_ref[...]
    ca = c * _sigmoid(c)
    o_ref[0] = jnp.dot(ca, w_ref[0], preferred_element_type=F32,
                       precision=lax.Precision.HIGHEST) + b_ref[0]


def _modulation(c, mod_w, mod_b):
    depth, d, n = mod_w.shape
    b = c.shape[0]
    tn = 1024
    return pl.pallas_call(
        _mod_kernel,
        grid=(depth, n // tn),
        in_specs=[
            pl.BlockSpec((b, d), lambda l, j: (0, 0)),
            pl.BlockSpec((1, d, tn), lambda l, j: (l, 0, j)),
            pl.BlockSpec((1, 1, tn), lambda l, j: (l, 0, j)),
        ],
        out_specs=pl.BlockSpec((1, b, tn), lambda l, j: (l, 0, j)),
        out_shape=jax.ShapeDtypeStruct((depth, b, n), F32),
        compiler_params=pltpu.CompilerParams(vmem_limit_bytes=VMEM_LIMIT),
        name="modulation",
    )(c, mod_w, mod_b.reshape(depth, 1, n))


def _rope_kernel(pos_ref, invf_ref, cos_ref, sin_ref):
    ang = pos_ref[0] * invf_ref[...]
    lane = lax.broadcasted_iota(jnp.int32, ang.shape, 1)
    rope = (lane >= ROPE_LO) & (lane < ROPE_LO + MLA_ROPE)
    cos_ref[0] = jnp.where(rope, jnp.cos(ang), jnp.where(lane < ROPE_LO, 1.0, 0.0))
    sin_ref[0] = jnp.where(rope, jnp.sin(ang), 0.0)


def _rope_tables(positions):
    b, s = positions.shape
    tm = 512
    half = MLA_ROPE // 2
    inv_freq = ROPE_THETA ** (-jnp.arange(0, MLA_ROPE, 2, dtype=F32) / MLA_ROPE)
    invf = jnp.zeros((1, LANES), F32)
    invf = invf.at[0, ROPE_LO:ROPE_LO + half].set(inv_freq)
    invf = invf.at[0, ROPE_LO + half:ROPE_LO + MLA_ROPE].set(inv_freq)
    pos = positions.astype(F32).reshape(b, s, 1)
    out = jax.ShapeDtypeStruct((b, s, LANES), F32)
    return pl.pallas_call(
        _rope_kernel,
        grid=(b, s // tm),
        in_specs=[pl.BlockSpec((1, tm, 1), lambda i, j: (i, j, 0)),
                  pl.BlockSpec((1, LANES), lambda i, j: (0, 0))],
        out_specs=[pl.BlockSpec((1, tm, LANES), lambda i, j: (i, j, 0))] * 2,
        out_shape=[out, out],
        name="rope_tables",
    )(pos, invf)


def _pre_kernel(x_ref, mod_ref, ng_ref, win_ref, qg_ref, kvg_ref, wuq_ref, wuqr_ref, wuk_ref,
                wuv_ref, cos_ref, sin_ref, lng_ref, lnb_ref, ws_ref, bs_ref, wg2_ref, bg_ref,
                q_out, k_out, v_out, sg_out, gq_out, gk_out, gla_out, gv_out, gr_out):
    tm = x_ref.shape[1]
    x = x_ref[0]
    sh1 = mod_ref[0, 0, 0:1, :]
    sc1 = mod_ref[0, 0, 1:2, :]
    h = _rms(x, ng_ref[0]) * (1.0 + sc1) + sh1
    z = _dot(h.astype(BF16), win_ref[0])

    cq = _rms(z[:, Z_CQ:Z_CQ + MLA_Q_LORA], qg_ref[0]).astype(BF16)
    ckv = _rms(z[:, Z_CKV:Z_CKV + MLA_KV_LORA], kvg_ref[0]).astype(BF16)
    cos = cos_ref[0]
    sin = sin_ref[0]
    qa = _dot(cq, wuq_ref[0])
    qb = _dot(cq, wuqr_ref[0])
    kn = _dot(ckv, wuk_ref[0])
    kr = z[:, Z_KR:Z_KR + LANES] * cos + z[:, Z_KRR:Z_KRR + LANES] * sin
    scale = float(MLA_NOPE + MLA_ROPE) ** -0.5 * LOG2_E
    for hd in range(MLA_HEADS):
        sl = slice(hd * HEAD_SLOT, (hd + 1) * HEAD_SLOT)
        q_out[0, :, sl] = ((qa[:, sl] * cos + qb[:, sl] * sin) * scale).astype(BF16)
        k_out[0, :, sl] = (kn[:, sl] + kr).astype(BF16)
    vt = _dot_nt(wuv_ref[0], ckv)
    vrow = lax.broadcasted_iota(jnp.int32, vt.shape, 0)
    v_out[0] = jnp.where(vrow % VT_ROWS == MLA_V, 1.0, vt).astype(BF16)

    u = _gelu_tanh(z[:, Z_SU:Z_SU + D_SG])
    gv = _gelu_tanh(z[:, Z_SV:Z_SV + D_SG])
    mu = jnp.mean(gv, axis=-1, keepdims=True)
    dv = gv - mu
    var = jnp.mean(dv * dv, axis=-1, keepdims=True)
    vn = (dv * lax.rsqrt(var + EPS) * lng_ref[0] + lnb_ref[0]).astype(BF16)
    trow = lax.broadcasted_iota(jnp.int32, (SG_CHUNK, SG_CHUNK), 0)
    tcol = lax.broadcasted_iota(jnp.int32, (SG_CHUNK, SG_CHUNK), 1)
    wmix = [jnp.where(trow >= tcol, ws_ref[0, i], 0.0).astype(BF16) for i in range(SG_HEADS)]
    lane = lax.broadcasted_iota(jnp.int32, (SG_CHUNK, LANES), 1)
    lo_half = lane < SG_HEAD_DIM
    for c in range(tm // SG_CHUNK):
        rows = slice(c * SG_CHUNK, (c + 1) * SG_CHUNK)
        for p in range(D_SG // LANES):
            cols = slice(p * LANES, (p + 1) * LANES)
            vp = vn[rows, cols]
            mixed = jnp.where(lo_half, _dot(wmix[2 * p], vp), _dot(wmix[2 * p + 1], vp))
            sg_out[0, rows, cols] = (u[rows, cols] * (mixed + bs_ref[0, :, cols])).astype(BF16)

    gq_out[0] = z[:, Z_GQ:Z_GQ + D_GQK] * (float(GLA_DK) ** -0.5)
    gk_out[0] = z[:, Z_GK:Z_GK + D_GQK]
    gv_out[0] = z[:, Z_GV:Z_GV + D_GLA]
    gate = _dot(z[:, Z_GLR:Z_GLR + LANES].astype(BF16), wg2_ref[0]) + bg_ref[0]
    gla_out[0] = _log_sigmoid(gate) * (1.0 / GLA_GATE_TAU)
    gr = z[:, Z_GR:Z_GR + D_GLA]
    gr_out[0] = gr * _sigmoid(gr)


def _pre_call(x, mod, w, cos, sin, l):
    b, s, d = x.shape
    tm = TM_PRE
    tok = lambda n, dt: jax.ShapeDtypeStruct((b, s, n), dt)
    tspec = lambda n: pl.BlockSpec((1, tm, n), lambda i, j: (i, j, 0))
    consts = [w["ng1"], w["win"], w["qg"], w["kvg"], w["wuq"], w["wuqr"], w["wuk"], w["wuv"]]
    consts2 = [w["lng"], w["lnb"], w["ws"], w["bs"], w["wg2"], w["bg"]]
    in_specs = ([tspec(d), _mod_spec(d, l)]
                + [_layer_spec(a.shape, l) for a in consts]
                + [tspec(LANES), tspec(LANES)]
                + [_layer_spec(a.shape, l) for a in consts2])
    outs = [(MLA_HEADS * HEAD_SLOT, BF16), (MLA_HEADS * HEAD_SLOT, BF16), (D_MLA, BF16),
            (D_SG, BF16), (D_GQK, F32), (D_GQK, F32), (D_GQK, F32), (D_GLA, F32), (D_GLA, F32)]
    out_specs = [tspec(n) for n, _ in outs]
    out_shape = [tok(n, dt) for n, dt in outs]
    out_specs[2] = pl.BlockSpec((1, MLA_HEADS * VT_ROWS, tm), lambda i, j: (i, 0, j))
    out_shape[2] = jax.ShapeDtypeStruct((b, MLA_HEADS * VT_ROWS, s), BF16)
    return pl.pallas_call(
        _pre_kernel,
        grid=(b, s // tm),
        in_specs=in_specs,
        out_specs=out_specs,
        out_shape=out_shape,
        compiler_params=pltpu.CompilerParams(
            dimension_semantics=("parallel", "parallel"), vmem_limit_bytes=VMEM_LIMIT),
        name="pre_mix",
    )(x, mod, *consts, cos, sin, *consts2)


def _attn_kernel(q_ref, k_ref, vt_ref, o_ref, m_s, acc_s, st_s):
    tq = q_ref.shape[1]
    tk = tq
    qi = pl.program_id(1)
    krow = lax.broadcasted_iota(jnp.int32, (tk, tq), 0)
    qcol = lax.broadcasted_iota(jnp.int32, (tk, tq), 1)
    causal = krow <= qcol

    m_s[...] = jnp.full(m_s.shape, NEG_BIG, F32)
    acc_s[...] = jnp.zeros(acc_s.shape, F32)

    def kv_step(j, masked):
        off = pl.multiple_of(j * tk, tk)
        def scores(hd):
            hcols = slice(hd * HEAD_SLOT, (hd + 1) * HEAD_SLOT)
            st = _dot_nt(k_ref[0, pl.ds(off, tk), hcols], q_ref[0, :, hcols])
            st_s[hd] = jnp.where(causal, st, NEG_BIG) if masked else st

        for hd in range(ATTN_LOOKAHEAD):
            scores(hd)
        for hd in range(MLA_HEADS):
            if hd + ATTN_LOOKAHEAD < MLA_HEADS:
                scores(hd + ATTN_LOOKAHEAD)
            m_old = m_s[hd]
            m_new = jnp.maximum(m_old, jnp.max(st_s[hd], axis=0, keepdims=True))
            alpha = jnp.exp2(m_old - m_new)
            m_s[hd] = m_new
            e = jnp.exp2((st_s[hd] - m_new).astype(BF16))
            vrows = slice(hd * VT_ROWS, (hd + 1) * VT_ROWS)
            pv = _dot(vt_ref[0, vrows, pl.ds(off, tk)], e)
            acc_s[vrows, :] = alpha * acc_s[vrows, :] + pv

    def body(j, carry):
        kv_step(j, masked=False)
        return carry

    lax.fori_loop(0, qi, body, 0)
    kv_step(qi, masked=True)

    for p in range(MLA_HEADS // 2):
        halves = []
        for hd in (2 * p, 2 * p + 1):
            r0 = hd * VT_ROWS
            halves.append(acc_s[r0:r0 + MLA_V, :] * (1.0 / acc_s[r0 + MLA_V:r0 + MLA_V + 1, :]))
        ot = jnp.concatenate(halves, axis=0)
        o_ref[0, :, p * LANES:(p + 1) * LANES] = ot.T.astype(BF16)


def _attn_call(q, k, vt):
    b, s, _ = q.shape
    tq = TQ_ATTN
    return pl.pallas_call(
        _attn_kernel,
        grid=(b, s // tq),
        in_specs=[pl.BlockSpec((1, tq, MLA_HEADS * HEAD_SLOT), lambda i, j: (i, j, 0)),
                  pl.BlockSpec((1, s, MLA_HEADS * HEAD_SLOT), lambda i, j: (i, 0, 0)),
                  pl.BlockSpec((1, MLA_HEADS * VT_ROWS, s), lambda i, j: (i, 0, 0))],
        out_specs=pl.BlockSpec((1, tq, D_MLA), lambda i, j: (i, j, 0)),
        out_shape=jax.ShapeDtypeStruct((b, s, D_MLA), BF16),
        scratch_shapes=[pltpu.VMEM((MLA_HEADS, 1, tq), F32),
                        pltpu.VMEM((MLA_HEADS * VT_ROWS, tq), F32),
                        pltpu.VMEM((MLA_HEADS, tq, tq), F32)],
        compiler_params=pltpu.CompilerParams(
            dimension_semantics=("parallel", "arbitrary"), vmem_limit_bytes=VMEM_LIMIT),
        name="mla_attention",
    )(q, k, vt)


def _gla_kernel(q_ref, k_ref, la_ref, v_ref, r_ref, g_ref, o_ref, st_ref, kpad, bpad, vpad):
    tg = q_ref.shape[1]
    ch = GLA_CHUNK
    sb = GLA_SUB

    @pl.when(pl.program_id(1) == 0)
    def _():
        st_ref[...] = jnp.zeros_like(st_ref)
        kpad[0:sb, :] = jnp.zeros((sb, D_GQK), F32)
        bpad[0:sb, :] = jnp.zeros((sb, D_GQK), F32)
        vpad[0:sb, :] = jnp.zeros((sb, D_GLA), F32)

    q = q_ref[0]
    k = k_ref[0]
    v = v_ref[0]
    ti = lax.broadcasted_iota(jnp.int32, (ch, ch), 0)
    tj = lax.broadcasted_iota(jnp.int32, (ch, ch), 1)
    tri = jnp.where(tj <= ti, 1.0, 0.0).astype(BF16)
    la = la_ref[0] * LOG2_E
    la_hi = la.astype(BF16)
    la_r1 = la - la_hi.astype(F32)
    la_mid = la_r1.astype(BF16)
    la_lo = (la_r1 - la_mid.astype(F32)).astype(BF16)
    b = jnp.concatenate(
        [_dot(tri, la_hi[c * ch:(c + 1) * ch]) + _dot(tri, la_mid[c * ch:(c + 1) * ch])
         + _dot(tri, la_lo[c * ch:(c + 1) * ch]) for c in range(tg // ch)], axis=0)
    kpad[sb:sb + tg, :] = k
    bpad[sb:sb + tg, :] = b
    vpad[sb:sb + tg, :] = v

    er = lax.broadcasted_iota(jnp.int32, (D_GQK, D_GLA), 0)
    ec = lax.broadcasted_iota(jnp.int32, (D_GQK, D_GLA), 1)
    head_sum = jnp.where((er // GLA_DK) == (ec // GLA_DV), 1.0, 0.0).astype(BF16)

    rowmod = lax.broadcasted_iota(jnp.int32, (tg, D_GQK), 0) % sb
    acc = jnp.zeros((tg, D_GLA), F32)
    for s in range(sb):
        ks = kpad[sb - s:sb - s + tg, :]
        bs = bpad[sb - s:sb - s + tg, :]
        vs = vpad[sb - s:sb - s + tg, :]
        p = jnp.where(rowmod >= s, q * ks * jnp.exp2(b - bs), 0.0)
        acc = acc + _dot(p.astype(BF16), head_sum) * vs

    sr = lax.broadcasted_iota(jnp.int32, (D_GLA, D_GQK), 0)
    sc = lax.broadcasted_iota(jnp.int32, (D_GLA, D_GQK), 1)
    same_head = (sr // GLA_DV) == (sc // GLA_DK)
    gr = lax.broadcasted_iota(jnp.int32, (D_GLA, D_GLA), 0)
    gc = lax.broadcasted_iota(jnp.int32, (D_GLA, D_GLA), 1)
    head_mean = jnp.where((gr // GLA_DV) == (gc // GLA_DV), 1.0 / GLA_DV, 0.0).astype(BF16)
    klane = lax.broadcasted_iota(jnp.int32, (sb, D_GQK), 1) // GLA_DK
    vlane = lax.broadcasted_iota(jnp.int32, (sb, D_GLA), 1) // GLA_DV
    krow = lax.broadcasted_iota(jnp.int32, (ch, D_GQK), 0)
    for c in range(tg // ch):
        rows = slice(c * ch, (c + 1) * ch)
        bc = b[rows]
        qc = q[rows]
        kc = k[rows]
        vc = v[rows].astype(BF16)

        pieces = [jnp.zeros((sb, D_GLA), F32)]
        for blk in range(1, ch // sb):
            lo_r = blk * sb
            r = bc[lo_r:lo_r + 1, :]
            qt = qc[lo_r:lo_r + sb] * jnp.exp2(bc[lo_r:lo_r + sb] - r)
            kt = (kc * jnp.exp2(jnp.where(krow < lo_r, r - bc, NEG_BIG))).astype(BF16)
            qexp = jnp.concatenate([jnp.where(klane == hd, qt, 0.0) for hd in range(GLA_HEADS)], axis=0)
            a = _dot_nt(qexp.astype(BF16), kt)
            oh = _dot(a.astype(BF16), vc)
            piece = jnp.zeros((sb, D_GLA), F32)
            for hd in range(GLA_HEADS):
                piece = piece + jnp.where(vlane == hd, oh[hd * sb:(hd + 1) * sb], 0.0)
            pieces.append(piece)
        o_sub = jnp.concatenate(pieces, axis=0)

        b_last = bc[ch - 1:ch, :]
        st = st_ref[...]
        o_inter = _dot_nt((qc * jnp.exp2(bc)).astype(BF16), st.astype(BF16))
        kd = (kc * jnp.exp2(b_last - bc)).astype(BF16)
        upd = _dot_tn(vc, kd)
        st_ref[...] = st * jnp.exp2(b_last) + jnp.where(same_head, upd, 0.0)
        o = acc[rows] + o_sub + o_inter
        o2 = o * o
        hi = o2.astype(BF16)
        lo = (o2 - hi.astype(F32)).astype(BF16)
        ms = _dot(hi, head_mean) + _dot(lo, head_mean)
        o_ref[0, rows, :] = (o * lax.rsqrt(ms + EPS) * g_ref[0] * r_ref[0, rows, :]).astype(BF16)


def _gla_call(gq, gk, gla, gv, gr, g256, l):
    b, s, _ = gq.shape
    tg = TG_GLA
    tspec = lambda n: pl.BlockSpec((1, tg, n), lambda i, j: (i, j, 0))
    return pl.pallas_call(
        _gla_kernel,
        grid=(b, s // tg),
        in_specs=[tspec(D_GQK), tspec(D_GQK), tspec(D_GQK), tspec(D_GLA), tspec(D_GLA),
                  _layer_spec(g256.shape, l)],
        out_specs=tspec(D_GLA),
        out_shape=jax.ShapeDtypeStruct((b, s, D_GLA), BF16),
        scratch_shapes=[pltpu.VMEM((D_GLA, D_GQK), F32),
                        pltpu.VMEM((GLA_SUB + tg, D_GQK), F32),
                        pltpu.VMEM((GLA_SUB + tg, D_GQK), F32),
                        pltpu.VMEM((GLA_SUB + tg, D_GLA), F32)],
        compiler_params=pltpu.CompilerParams(
            dimension_semantics=("parallel", "arbitrary"), vmem_limit_bytes=VMEM_LIMIT),
        name="gla",
    )(gq, gk, gla, gv, gr, g256)


def _post_kernel(x_ref, oa_ref, ob_ref, oc_ref, mod_ref, wo_ref, ng_ref, wup_ref, cw_ref, cb_ref,
                 wdn_ref, fg_ref, o_ref, zbuf, act, *, final):
    tm = x_ref.shape[1]
    halo = 8
    g1 = mod_ref[0, 0, 2:3, :]
    sh2 = mod_ref[0, 0, 3:4, :]
    sc2 = mod_ref[0, 0, 4:5, :]
    g2 = mod_ref[0, 0, 5:6, :]
    mix = (_dot(oa_ref[0], wo_ref[0, 0:D_MLA, :])
           + _dot(ob_ref[0], wo_ref[0, D_MLA:D_MLA + D_SG, :])
           + _dot(oc_ref[0], wo_ref[0, D_MLA + D_SG:, :]))
    x1 = x_ref[0] + g1 * mix
    h2 = _rms(x1, ng_ref[0]) * (1.0 + sc2) + sh2

    @pl.when(pl.program_id(1) == 0)
    def _():
        zbuf[0:halo, :] = jnp.zeros((halo, 2 * D_FF), F32)

    zbuf[halo:halo + tm, :] = _dot(h2.astype(BF16), wup_ref[0])
    cc = 256
    for j in range(D_FF // cc):
        parts = []
        for cols in (slice(j * cc, (j + 1) * cc), slice(D_FF + j * cc, D_FF + (j + 1) * cc)):
            zc = cb_ref[0, :, cols] + cw_ref[0, 2:3, cols] * zbuf[halo:halo + tm, cols]
            zc = zc + cw_ref[0, 0:1, cols] * zbuf[halo - 2:halo - 2 + tm, cols]
            zc = zc + cw_ref[0, 1:2, cols] * zbuf[halo - 1:halo - 1 + tm, cols]
            parts.append(zc)
        val, gate = parts
        act[:, j * cc:(j + 1) * cc] = (gate * _sigmoid(gate) * val).astype(BF16)
    zbuf[halo - 2:halo, :] = zbuf[halo + tm - 2:halo + tm, :]
    x2 = x1 + g2 * _dot(act[...], wdn_ref[0])
    if final:
        x2 = _rms(x2, fg_ref[...])
    o_ref[0] = x2


def _post_call(x, oa, ob, oc, mod, w, final_g, l, final):
    b, s, d = x.shape
    tm = TM_POST
    tspec = lambda n: pl.BlockSpec((1, tm, n), lambda i, j: (i, j, 0))
    consts = [w["wo"], w["ng2"], w["wup"], w["cw"], w["cb"], w["wdn"]]
    return pl.pallas_call(
        functools.partial(_post_kernel, final=final),
        grid=(b, s // tm),
        in_specs=[tspec(d), tspec(D_MLA), tspec(D_SG), tspec(D_GLA), _mod_spec(d, l)]
                 + [_layer_spec(a.shape, l) for a in consts] + [_const_spec(final_g.shape)],
        out_specs=tspec(d),
        out_shape=jax.ShapeDtypeStruct((b, s, d), F32),
        scratch_shapes=[pltpu.VMEM((8 + tm, 2 * D_FF), F32), pltpu.VMEM((tm, D_FF), BF16)],
        compiler_params=pltpu.CompilerParams(
            dimension_semantics=("parallel", "arbitrary"), vmem_limit_bytes=VMEM_LIMIT),
        name="post_ffn",
    )(x, oa, ob, oc, mod, *consts, final_g)


def _rot_half_cols(w):
    half = MLA_ROPE // 2
    return jnp.concatenate([-w[..., half:], w[..., :half]], axis=-1)


def _stacked_weights(p):
    depth, d, _ = p["w_in"].shape
    splits = (MLA_Q_LORA, MLA_KV_LORA, MLA_ROPE, D_SG, D_SG, D_GQK, D_GQK, D_GLA, GLA_GATE_RANK, D_GLA)
    offs = [0]
    for n in splits:
        offs.append(offs[-1] + n)
    w_in = p["w_in"].astype(BF16)
    col = lambda i: w_in[..., offs[i]:offs[i + 1]]
    zeros = lambda n: jnp.zeros((depth, d, n), BF16)
    rope_pad = LANES - ROPE_LO - MLA_ROPE
    win = jnp.concatenate([
        col(0), col(1),
        zeros(ROPE_LO), col(2), zeros(rope_pad),
        zeros(ROPE_LO), _rot_half_cols(col(2)), zeros(rope_pad),
        col(3), col(4), col(5), col(6), col(7),
        col(8), zeros(LANES - GLA_GATE_RANK), col(9)], axis=-1)
    assert win.shape[-1] == Z_COLS

    keep3 = ((0, 0), (0, 0), (0, 0))
    wq = p["mla_w_uq"].astype(BF16).reshape(depth, MLA_Q_LORA, MLA_HEADS, MLA_NOPE + MLA_ROPE)
    wuq = jnp.pad(wq, keep3 + ((0, rope_pad),)).reshape(depth, MLA_Q_LORA, MLA_HEADS * HEAD_SLOT)
    wqr = jnp.pad(_rot_half_cols(wq[..., MLA_NOPE:]), keep3 + ((ROPE_LO, rope_pad),))
    wuqr = wqr.reshape(depth, MLA_Q_LORA, MLA_HEADS * HEAD_SLOT)
    wkv = p["mla_w_ukv"].astype(BF16).reshape(depth, MLA_KV_LORA, MLA_HEADS, MLA_NOPE + MLA_V)
    wuk = jnp.pad(wkv[..., :MLA_NOPE], keep3 + ((0, HEAD_SLOT - MLA_NOPE),))
    wuk = wuk.reshape(depth, MLA_KV_LORA, MLA_HEADS * HEAD_SLOT)
    wuv = jnp.pad(wkv[..., MLA_NOPE:], keep3 + ((0, VT_ROWS - MLA_V),))
    wuv = jnp.swapaxes(wuv.reshape(depth, MLA_KV_LORA, MLA_HEADS * VT_ROWS), 1, 2)

    bs = jnp.repeat(jnp.swapaxes(p["sg_b_s"], 1, 2), SG_HEAD_DIM, axis=2)
    wg2 = jnp.pad(p["gla_w_gate2"].astype(BF16), ((0, 0), (0, LANES - GLA_GATE_RANK), (0, 0)))
    row = lambda a: a.reshape(depth, 1, -1)
    return {
        "ng1": row(p["norm_mix_g"]), "win": win,
        "qg": row(p["mla_q_norm_g"]), "kvg": row(p["mla_kv_norm_g"]),
        "wuq": wuq, "wuqr": wuqr, "wuk": wuk, "wuv": wuv,
        "lng": row(p["sg_ln_g"]), "lnb": row(p["sg_ln_b"]), "ws": p["sg_w_s"], "bs": bs,
        "wg2": wg2, "bg": row(p["gla_b_gate"]),
        "gng": row(jnp.tile(p["gla_norm_g"], (1, GLA_HEADS))),
        "wo": p["w_out"].astype(BF16), "ng2": row(p["norm_ffn_g"]),
        "wup": p["ffn_w_up"].astype(BF16), "cw": p["ffn_conv_w"],
        "cb": row(p["ffn_conv_b"]), "wdn": p["ffn_w_down"].astype(BF16),
    }


def kernel(x, c, positions, mod_w, mod_b, norm_mix_g, norm_ffn_g, w_in, mla_q_norm_g, mla_kv_norm_g, mla_w_uq, mla_w_ukv, sg_ln_g, sg_ln_b, sg_w_s, sg_b_s, gla_w_gate2, gla_b_gate, gla_norm_g, w_out, ffn_w_up, ffn_conv_w, ffn_conv_b, ffn_w_down, final_norm_g):
    p = dict(norm_mix_g=norm_mix_g, norm_ffn_g=norm_ffn_g, w_in=w_in, mla_q_norm_g=mla_q_norm_g,
             mla_kv_norm_g=mla_kv_norm_g, mla_w_uq=mla_w_uq, mla_w_ukv=mla_w_ukv, sg_ln_g=sg_ln_g,
             sg_ln_b=sg_ln_b, sg_w_s=sg_w_s, sg_b_s=sg_b_s, gla_w_gate2=gla_w_gate2,
             gla_b_gate=gla_b_gate, gla_norm_g=gla_norm_g, w_out=w_out, ffn_w_up=ffn_w_up,
             ffn_conv_w=ffn_conv_w, ffn_conv_b=ffn_conv_b, ffn_w_down=ffn_w_down)
    depth = mod_w.shape[0]
    b = x.shape[0]
    mod = _modulation(c, mod_w, mod_b).reshape(depth, b, 6, D_MODEL)
    cos, sin = _rope_tables(positions)
    fg = final_norm_g.reshape(1, -1)
    w = _stacked_weights(p)
    for l in range(depth):
        q, k, v, o_sg, gq, gk, gla, gv, gr = _pre_call(x, mod, w, cos, sin, l)
        o_mla = _attn_call(q, k, v)
        o_gla = _gla_call(gq, gk, gla, gv, gr, w["gng"], l)
        x = _post_call(x, o_mla, o_sg, o_gla, mod, w, fg, l, final=(l == depth - 1))
    return x
```

```python
import functools

import jax
import jax.numpy as jnp
from jax import lax
from jax.experimental import pallas as pl
from jax.experimental.pallas import tpu as pltpu

D_MODEL = 1024
MLA_HEADS = 8
MLA_NOPE = 64
MLA_ROPE = 32
MLA_V = 64
MLA_Q_LORA = 256
MLA_KV_LORA = 128
ROPE_THETA = 10000.0
SG_HEADS = 4
SG_HEAD_DIM = 64
SG_CHUNK = 128
GLA_HEADS = 4
GLA_DK = 32
GLA_DV = 64
GLA_GATE_RANK = 16
GLA_GATE_TAU = 16.0
GLA_CHUNK = 64
GLA_SUB = 16
D_FF = 2816
CONV_WIDTH = 3
EPS = 1e-6

D_MLA = MLA_HEADS * MLA_V
D_SG = SG_HEADS * SG_HEAD_DIM
D_GLA = GLA_HEADS * GLA_DV
D_GQK = GLA_HEADS * GLA_DK

LANES = 128
HEAD_SLOT = LANES
ROPE_LO = MLA_NOPE
VT_ROWS = MLA_V + 16
VMEM_LIMIT = 56 * 1024 * 1024

Z_CQ = 0
Z_CKV = Z_CQ + MLA_Q_LORA
Z_KR = Z_CKV + MLA_KV_LORA
Z_KRR = Z_KR + LANES
Z_SU = Z_KRR + LANES
Z_SV = Z_SU + D_SG
Z_GQ = Z_SV + D_SG
Z_GK = Z_GQ + D_GQK
Z_GV = Z_GK + D_GQK
Z_GLR = Z_GV + D_GLA
Z_GR = Z_GLR + LANES
Z_COLS = Z_GR + D_GLA

TM_PRE = 512
TQ_ATTN = 512
TG_GLA = 512
ATTN_LOOKAHEAD = 2
TM_POST = 512
FFN_CHUNK = 256

F32 = jnp.float32
BF16 = jnp.bfloat16
NEG_BIG = -1e30
LOG2_E = 1.4426950408889634


def _dot(a, b):
    return jnp.dot(a, b, preferred_element_type=F32)


def _dot_nt(a, b):
    return lax.dot_general(a, b, (((1,), (1,)), ((), ())), preferred_element_type=F32)


def _dot_tn(a, b):
    return lax.dot_general(a, b, (((0,), (0,)), ((), ())), preferred_element_type=F32)


def _sigmoid(x):
    return 1.0 / (1.0 + jnp.exp(-x))


def _gelu_tanh(x):
    return 0.5 * x * (1.0 + jnp.tanh(0.7978845608028654 * (x + 0.044715 * (x * x * x))))


def _log_sigmoid(x):
    return jnp.minimum(x, 0.0) - jnp.log1p(jnp.exp(-jnp.abs(x)))


def _rms(x, g):
    return x * lax.rsqrt(jnp.mean(x * x, axis=-1, keepdims=True) + EPS) * g


def _const_spec(shape):
    zeros = (0,) * len(shape)
    return pl.BlockSpec(shape, lambda *_: zeros, pipeline_mode=pl.Buffered(1))


def _layer_spec(shape, l):
    idx = (l,) + (0,) * (len(shape) - 1)
    return pl.BlockSpec((1,) + tuple(shape[1:]), lambda *_: idx, pipeline_mode=pl.Buffered(1))


def _mod_spec(d, l):
    return pl.BlockSpec((1, 1, 6, d), lambda i, j: (l, i, 0, 0))


def _mod_kernel(c_ref, w_ref, b_ref, o_ref):
    c = c_ref[...]
    ca = c * _sigmoid(c)
    o_ref[0] = jnp.dot(ca, w_ref[0], preferred_element_type=F32,
                       precision=lax.Precision.HIGHEST) + b_ref[0]


def _modulation(c, mod_w, mod_b):
    depth, d, n = mod_w.shape
    b = c.shape[0]
    tn = 1024
    return pl.pallas_call(
        _mod_kernel,
        grid=(depth, n // tn),
        in_specs=[
            pl.BlockSpec((b, d), lambda l, j: (0, 0)),
            pl.BlockSpec((1, d, tn), lambda l, j: (l, 0, j)),
            pl.BlockSpec((1, 1, tn), lambda l, j: (l, 0, j)),
        ],
        out_specs=pl.BlockSpec((1, b, tn), lambda l, j: (l, 0, j)),
        out_shape=jax.ShapeDtypeStruct((depth, b, n), F32),
        compiler_params=pltpu.CompilerParams(vmem_limit_bytes=VMEM_LIMIT),
        name="modulation",
    )(c, mod_w, mod_b.reshape(depth, 1, n))


def _rope_kernel(pos_ref, post_ref, invf_ref, invft_ref, cos_ref, sin_ref, cost_ref, sint_ref):
    ang = pos_ref[0] * invf_ref[...]
    lane = lax.broadcasted_iota(jnp.int32, ang.shape, 1)
    rope = (lane >= ROPE_LO) & (lane < ROPE_LO + MLA_ROPE)
    cos_ref[0] = jnp.where(rope, jnp.cos(ang), 0.0)
    sin_ref[0] = jnp.where(rope, jnp.sin(ang), 0.0)
    angt = invft_ref[...] * post_ref[0]
    cost_ref[0] = jnp.cos(angt)
    sint_ref[0] = jnp.sin(angt)


def _rope_tables(positions):
    b, s = positions.shape
    tm = 512
    inv_freq = ROPE_THETA ** (-jnp.arange(0, MLA_ROPE, 2, dtype=F32) / MLA_ROPE)
    inv2 = jnp.concatenate([inv_freq, inv_freq])
    invf = jnp.zeros((1, LANES), F32).at[0, ROPE_LO:ROPE_LO + MLA_ROPE].set(inv2)
    posf = positions.astype(F32)
    out = jax.ShapeDtypeStruct((b, s, LANES), F32)
    outt = jax.ShapeDtypeStruct((b, MLA_ROPE, s), F32)
    return pl.pallas_call(
        _rope_kernel,
        grid=(b, s // tm),
        in_specs=[pl.BlockSpec((1, tm, 1), lambda i, j: (i, j, 0)),
                  pl.BlockSpec((1, 1, tm), lambda i, j: (i, 0, j)),
                  pl.BlockSpec((1, LANES), lambda i, j: (0, 0)),
                  pl.BlockSpec((MLA_ROPE, 1), lambda i, j: (0, 0))],
        out_specs=[pl.BlockSpec((1, tm, LANES), lambda i, j: (i, j, 0))] * 2
                  + [pl.BlockSpec((1, MLA_ROPE, tm), lambda i, j: (i, 0, j))] * 2,
        out_shape=[out, out, outt, outt],
        name="rope_tables",
    )(posf.reshape(b, s, 1), posf.reshape(b, 1, s), invf, inv2.reshape(MLA_ROPE, 1))


def _pre_kernel(x_ref, mod_ref, ng_ref, win_ref, qg_ref, kvg_ref, wuq_ref, wuqr_ref, wuk_ref,
                wuv_ref, cos_ref, sin_ref, cost_ref, sint_ref, lng_ref, lnb_ref, ws_ref, bs_ref,
                wg2_ref, bg_ref,
                q_out, k_out, v_out, sg_out, gq_out, gk_out, gla_out, gv_out, gr_out):
    tm = x_ref.shape[1]
    x = x_ref[0]
    sh1 = mod_ref[0, 0, 0:1, :]
    sc1 = mod_ref[0, 0, 1:2, :]
    h = _rms(x, ng_ref[0]) * (1.0 + sc1) + sh1
    z = _dot(h.astype(BF16), win_ref[0])

    cq = _rms(z[:, Z_CQ:Z_CQ + MLA_Q_LORA], qg_ref[0]).astype(BF16)
    ckv = _rms(z[:, Z_CKV:Z_CKV + MLA_KV_LORA], kvg_ref[0]).astype(BF16)
    qat = _dot_nt(wuq_ref[0], cq)
    qrt = _dot_nt(wuqr_ref[0], cq)
    cost = cost_ref[0]
    sint = sint_ref[0]
    scale = float(MLA_NOPE + MLA_ROPE) ** -0.5 * LOG2_E
    zero_rows = jnp.zeros((HEAD_SLOT - ROPE_LO - MLA_ROPE, tm), BF16)
    for hd in range(MLA_HEADS):
        r0 = hd * HEAD_SLOT
        rot = qrt[hd * MLA_ROPE:(hd + 1) * MLA_ROPE]
        q_out[0, r0:r0 + ROPE_LO, :] = (qat[r0:r0 + ROPE_LO] * scale).astype(BF16)
        q_out[0, r0 + ROPE_LO:r0 + ROPE_LO + MLA_ROPE, :] = (
            (qat[r0 + ROPE_LO:r0 + ROPE_LO + MLA_ROPE] * cost + rot * sint) * scale).astype(BF16)
        q_out[0, r0 + ROPE_LO + MLA_ROPE:r0 + HEAD_SLOT, :] = zero_rows
    kn = _dot(ckv, wuk_ref[0])
    kr = z[:, Z_KR:Z_KR + LANES] * cos_ref[0] + z[:, Z_KRR:Z_KRR + LANES] * sin_ref[0]
    for hd in range(MLA_HEADS):
        sl = slice(hd * HEAD_SLOT, (hd + 1) * HEAD_SLOT)
        k_out[0, :, sl] = (kn[:, sl] + kr).astype(BF16)
    vt = _dot_nt(wuv_ref[0], ckv)
    vrow = lax.broadcasted_iota(jnp.int32, vt.shape, 0)
    v_out[0] = jnp.where(vrow % VT_ROWS == MLA_V, 1.0, vt).astype(BF16)

    u = _gelu_tanh(z[:, Z_SU:Z_SU + D_SG])
    gv = _gelu_tanh(z[:, Z_SV:Z_SV + D_SG])
    mu = jnp.mean(gv, axis=-1, keepdims=True)
    dv = gv - mu
    var = jnp.mean(dv * dv, axis=-1, keepdims=True)
    vn = (dv * lax.rsqrt(var + EPS) * lng_ref[0] + lnb_ref[0]).astype(BF16)
    trow = lax.broadcasted_iota(jnp.int32, (SG_CHUNK, SG_CHUNK), 0)
    tcol = lax.broadcasted_iota(jnp.int32, (SG_CHUNK, SG_CHUNK), 1)
    wmix = [jnp.where(trow >= tcol, ws_ref[0, i], 0.0).astype(BF16) for i in range(SG_HEADS)]
    lane = lax.broadcasted_iota(jnp.int32, (SG_CHUNK, LANES), 1)
    lo_half = lane < SG_HEAD_DIM
    for c in range(tm // SG_CHUNK):
        rows = slice(c * SG_CHUNK, (c + 1) * SG_CHUNK)
        for p in range(D_SG // LANES):
            cols = slice(p * LANES, (p + 1) * LANES)
            vp = vn[rows, cols]
            mixed = jnp.where(lo_half, _dot(wmix[2 * p], vp), _dot(wmix[2 * p + 1], vp))
            sg_out[0, rows, cols] = (u[rows, cols] * (mixed + bs_ref[0, :, cols])).astype(BF16)

    gq_out[0] = z[:, Z_GQ:Z_GQ + D_GQK] * (float(GLA_DK) ** -0.5)
    gk_out[0] = z[:, Z_GK:Z_GK + D_GQK]
    gv_out[0] = z[:, Z_GV:Z_GV + D_GLA]
    gate = _dot(z[:, Z_GLR:Z_GLR + LANES].astype(BF16), wg2_ref[0]) + bg_ref[0]
    gla_out[0] = _log_sigmoid(gate) * (1.0 / GLA_GATE_TAU)
    gr = z[:, Z_GR:Z_GR + D_GLA]
    gr_out[0] = gr * _sigmoid(gr)


def _pre_call(x, mod, w, tables, l):
    b, s, d = x.shape
    tm = TM_PRE
    tok = lambda n, dt: jax.ShapeDtypeStruct((b, s, n), dt)
    tspec = lambda n: pl.BlockSpec((1, tm, n), lambda i, j: (i, j, 0))
    tspec_t = lambda n: pl.BlockSpec((1, n, tm), lambda i, j: (i, 0, j))
    consts = [w["ng1"], w["win"], w["qg"], w["kvg"], w["wuq"], w["wuqr"], w["wuk"], w["wuv"]]
    consts2 = [w["lng"], w["lnb"], w["ws"], w["bs"], w["wg2"], w["bg"]]
    in_specs = ([tspec(d), _mod_spec(d, l)]
                + [_layer_spec(a.shape, l) for a in consts]
                + [tspec(LANES), tspec(LANES), tspec_t(MLA_ROPE), tspec_t(MLA_ROPE)]
                + [_layer_spec(a.shape, l) for a in consts2])
    outs = [(MLA_HEADS * HEAD_SLOT, BF16), (MLA_HEADS * HEAD_SLOT, BF16), (D_MLA, BF16),
            (D_SG, BF16), (D_GQK, F32), (D_GQK, F32), (D_GQK, F32), (D_GLA, F32), (D_GLA, F32)]
    out_specs = [tspec(n) for n, _ in outs]
    out_shape = [tok(n, dt) for n, dt in outs]
    out_specs[0] = tspec_t(MLA_HEADS * HEAD_SLOT)
    out_shape[0] = jax.ShapeDtypeStruct((b, MLA_HEADS * HEAD_SLOT, s), BF16)
    out_specs[2] = tspec_t(MLA_HEADS * VT_ROWS)
    out_shape[2] = jax.ShapeDtypeStruct((b, MLA_HEADS * VT_ROWS, s), BF16)
    return pl.pallas_call(
        _pre_kernel,
        grid=(b, s // tm),
        in_specs=in_specs,
        out_specs=out_specs,
        out_shape=out_shape,
        compiler_params=pltpu.CompilerParams(
            dimension_semantics=("parallel", "parallel"), vmem_limit_bytes=VMEM_LIMIT),
        name="pre_mix",
    )(x, mod, *consts, *tables, *consts2)


def _attn_kernel(q_ref, k_ref, vt_ref, o_ref, m_s, acc_s, st_s):
    tq = q_ref.shape[2]
    tk = tq
    qi = pl.program_id(1)
    krow = lax.broadcasted_iota(jnp.int32, (tk, tq), 0)
    qcol = lax.broadcasted_iota(jnp.int32, (tk, tq), 1)
    causal = krow <= qcol

    m_s[...] = jnp.full(m_s.shape, NEG_BIG, F32)
    acc_s[...] = jnp.zeros(acc_s.shape, F32)

    def kv_step(j, masked):
        off = pl.multiple_of(j * tk, tk)

        def scores(hd):
            hcols = slice(hd * HEAD_SLOT, (hd + 1) * HEAD_SLOT)
            st = _dot(k_ref[0, pl.ds(off, tk), hcols], q_ref[0, hcols, :])
            st_s[hd] = jnp.where(causal, st, NEG_BIG) if masked else st

        for hd in range(ATTN_LOOKAHEAD):
            scores(hd)
        for hd in range(MLA_HEADS):
            if hd + ATTN_LOOKAHEAD < MLA_HEADS:
                scores(hd + ATTN_LOOKAHEAD)
            m_old = m_s[hd]
            m_new = jnp.maximum(m_old, jnp.max(st_s[hd], axis=0, keepdims=True))
            alpha = jnp.exp2(m_old - m_new)
            m_s[hd] = m_new
            e = jnp.exp2((st_s[hd] - m_new).astype(BF16))
            vrows = slice(hd * VT_ROWS, (hd + 1) * VT_ROWS)
            pv = _dot(vt_ref[0, vrows, pl.ds(off, tk)], e)
            acc_s[vrows, :] = alpha * acc_s[vrows, :] + pv

    def body(j, carry):
        kv_step(j, masked=False)
        return carry

    lax.fori_loop(0, qi, body, 0)
    kv_step(qi, masked=True)

    for p in range(MLA_HEADS // 2):
        halves = []
        for hd in (2 * p, 2 * p + 1):
            r0 = hd * VT_ROWS
            halves.append(acc_s[r0:r0 + MLA_V, :] * (1.0 / acc_s[r0 + MLA_V:r0 + MLA_V + 1, :]))
        ot = jnp.concatenate(halves, axis=0)
        o_ref[0, :, p * LANES:(p + 1) * LANES] = ot.T.astype(BF16)


def _attn_call(qt, k, vt):
    b, s, _ = k.shape
    tq = TQ_ATTN
    return pl.pallas_call(
        _attn_kernel,
        grid=(b, s // tq),
        in_specs=[pl.BlockSpec((1, MLA_HEADS * HEAD_SLOT, tq), lambda i, j: (i, 0, j)),
                  pl.BlockSpec((1, s, MLA_HEADS * HEAD_SLOT), lambda i, j: (i, 0, 0)),
                  pl.BlockSpec((1, MLA_HEADS * VT_ROWS, s), lambda i, j: (i, 0, 0))],
        out_specs=pl.BlockSpec((1, tq, D_MLA), lambda i, j: (i, j, 0)),
        out_shape=jax.ShapeDtypeStruct((b, s, D_MLA), BF16),
        scratch_shapes=[pltpu.VMEM((MLA_HEADS, 1, tq), F32),
                        pltpu.VMEM((MLA_HEADS * VT_ROWS, tq), F32),
                        pltpu.VMEM((MLA_HEADS, tq, tq), F32)],
        compiler_params=pltpu.CompilerParams(
            dimension_semantics=("parallel", "arbitrary"), vmem_limit_bytes=VMEM_LIMIT),
        name="mla_attention",
    )(qt, k, vt)


def _gla_kernel(q_ref, k_ref, la_ref, v_ref, r_ref, g_ref, o_ref, st_ref, kpad, bpad, vpad):
    tg = q_ref.shape[1]
    ch = GLA_CHUNK
    sb = GLA_SUB

    @pl.when(pl.program_id(1) == 0)
    def _():
        st_ref[...] = jnp.zeros_like(st_ref)
        kpad[0:sb, :] = jnp.zeros((sb, D_GQK), F32)
        bpad[0:sb, :] = jnp.zeros((sb, D_GQK), F32)
        vpad[0:sb, :] = jnp.zeros((sb, D_GLA), F32)

    q = q_ref[0]
    k = k_ref[0]
    v = v_ref[0]
    ti = lax.broadcasted_iota(jnp.int32, (ch, ch), 0)
    tj = lax.broadcasted_iota(jnp.int32, (ch, ch), 1)
    tri = jnp.where(tj <= ti, 1.0, 0.0).astype(BF16)
    la = la_ref[0] * LOG2_E
    la_hi = la.astype(BF16)
    la_r1 = la - la_hi.astype(F32)
    la_mid = la_r1.astype(BF16)
    la_lo = (la_r1 - la_mid.astype(F32)).astype(BF16)
    b = jnp.concatenate(
        [_dot(tri, la_hi[c * ch:(c + 1) * ch]) + _dot(tri, la_mid[c * ch:(c + 1) * ch])
         + _dot(tri, la_lo[c * ch:(c + 1) * ch]) for c in range(tg // ch)], axis=0)
    kpad[sb:sb + tg, :] = k
    bpad[sb:sb + tg, :] = b
    vpad[sb:sb + tg, :] = v

    er = lax.broadcasted_iota(jnp.int32, (D_GQK, D_GLA), 0)
    ec = lax.broadcasted_iota(jnp.int32, (D_GQK, D_GLA), 1)
    head_sum = jnp.where((er // GLA_DK) == (ec // GLA_DV), 1.0, 0.0).astype(BF16)

    rowmod = lax.broadcasted_iota(jnp.int32, (tg, D_GQK), 0) % sb
    acc = jnp.zeros((tg, D_GLA), F32)
    for s in range(sb):
        ks = kpad[sb - s:sb - s + tg, :]
        bs = bpad[sb - s:sb - s + tg, :]
        vs = vpad[sb - s:sb - s + tg, :]
        p = jnp.where(rowmod >= s, q * ks * jnp.exp2(b - bs), 0.0)
        acc = acc + _dot(p.astype(BF16), head_sum) * vs

    sr = lax.broadcasted_iota(jnp.int32, (D_GLA, D_GQK), 0)
    sc = lax.broadcasted_iota(jnp.int32, (D_GLA, D_GQK), 1)
    same_head = (sr // GLA_DV) == (sc // GLA_DK)
    gr = lax.broadcasted_iota(jnp.int32, (D_GLA, D_GLA), 0)
    gc = lax.broadcasted_iota(jnp.int32, (D_GLA, D_GLA), 1)
    head_mean = jnp.where((gr // GLA_DV) == (gc // GLA_DV), 1.0 / GLA_DV, 0.0).astype(BF16)
    klane = lax.broadcasted_iota(jnp.int32, (sb, D_GQK), 1) // GLA_DK
    vlane = lax.broadcasted_iota(jnp.int32, (sb, D_GLA), 1) // GLA_DV
    krow = lax.broadcasted_iota(jnp.int32, (ch, D_GQK), 0)
    for c in range(tg // ch):
        rows = slice(c * ch, (c + 1) * ch)
        bc = b[rows]
        qc = q[rows]
        kc = k[rows]
        vc = v[rows].astype(BF16)

        pieces = [jnp.zeros((sb, D_GLA), F32)]
        for blk in range(1, ch // sb):
            lo_r = blk * sb
            r = bc[lo_r:lo_r + 1, :]
            qt = qc[lo_r:lo_r + sb] * jnp.exp2(bc[lo_r:lo_r + sb] - r)
            kt = (kc * jnp.exp2(jnp.where(krow < lo_r, r - bc, NEG_BIG))).astype(BF16)
            qexp = jnp.concatenate([jnp.where(klane == hd, qt, 0.0) for hd in range(GLA_HEADS)], axis=0)
            a = _dot_nt(qexp.astype(BF16), kt)
            oh = _dot(a.astype(BF16), vc)
            piece = jnp.zeros((sb, D_GLA), F32)
            for hd in range(GLA_HEADS):
                piece = piece + jnp.where(vlane == hd, oh[hd * sb:(hd + 1) * sb], 0.0)
            pieces.append(piece)
        o_sub = jnp.concatenate(pieces, axis=0)

        b_last = bc[ch - 1:ch, :]
        st = st_ref[...]
        o_inter = _dot_nt((qc * jnp.exp2(bc)).astype(BF16), st.astype(BF16))
        kd = (kc * jnp.exp2(b_last - bc)).astype(BF16)
        upd = _dot_tn(vc, kd)
        st_ref[...] = st * jnp.exp2(b_last) + jnp.where(same_head, upd, 0.0)
        o = acc[rows] + o_sub + o_inter
        o2 = o * o
        hi = o2.astype(BF16)
        lo = (o2 - hi.astype(F32)).astype(BF16)
        ms = _dot(hi, head_mean) + _dot(lo, head_mean)
        o_ref[0, rows, :] = (o * lax.rsqrt(ms + EPS) * g_ref[0] * r_ref[0, rows, :]).astype(BF16)


def _gla_call(gq, gk, gla, gv, gr, g256, l):
    b, s, _ = gq.shape
    tg = TG_GLA
    tspec = lambda n: pl.BlockSpec((1, tg, n), lambda i, j: (i, j, 0))
    return pl.pallas_call(
        _gla_kernel,
        grid=(b, s // tg),
        in_specs=[tspec(D_GQK), tspec(D_GQK), tspec(D_GQK), tspec(D_GLA), tspec(D_GLA),
                  _layer_spec(g256.shape, l)],
        out_specs=tspec(D_GLA),
        out_shape=jax.ShapeDtypeStruct((b, s, D_GLA), BF16),
        scratch_shapes=[pltpu.VMEM((D_GLA, D_GQK), F32),
                        pltpu.VMEM((GLA_SUB + tg, D_GQK), F32),
                        pltpu.VMEM((GLA_SUB + tg, D_GQK), F32),
                        pltpu.VMEM((GLA_SUB + tg, D_GLA), F32)],
        compiler_params=pltpu.CompilerParams(
            dimension_semantics=("parallel", "arbitrary"), vmem_limit_bytes=VMEM_LIMIT),
        name="gla",
    )(gq, gk, gla, gv, gr, g256)


def _post_kernel(x_ref, oa_ref, ob_ref, oc_ref, mod_ref, wo_ref, ng_ref, wup_ref, cw_ref, cb_ref,
                 wdn_ref, fg_ref, o_ref, zbuf, act, *, final):
    tm = x_ref.shape[1]
    halo = 8
    g1 = mod_ref[0, 0, 2:3, :]
    sh2 = mod_ref[0, 0, 3:4, :]
    sc2 = mod_ref[0, 0, 4:5, :]
    g2 = mod_ref[0, 0, 5:6, :]
    mix = (_dot(oa_ref[0], wo_ref[0, 0:D_MLA, :])
           + _dot(ob_ref[0], wo_ref[0, D_MLA:D_MLA + D_SG, :])
           + _dot(oc_ref[0], wo_ref[0, D_MLA + D_SG:, :]))
    x1 = x_ref[0] + g1 * mix
    h2 = _rms(x1, ng_ref[0]) * (1.0 + sc2) + sh2

    @pl.when(pl.program_id(1) == 0)
    def _():
        zbuf[0:halo, :] = jnp.zeros((halo, 2 * D_FF), F32)

    zbuf[halo:halo + tm, :] = _dot(h2.astype(BF16), wup_ref[0])
    cc = FFN_CHUNK
    for j in range(D_FF // cc):
        parts = []
        for cols in (slice(j * cc, (j + 1) * cc), slice(D_FF + j * cc, D_FF + (j + 1) * cc)):
            zc = cb_ref[0, :, cols] + cw_ref[0, 2:3, cols] * zbuf[halo:halo + tm, cols]
            zc = zc + cw_ref[0, 0:1, cols] * zbuf[halo - 2:halo - 2 + tm, cols]
            zc = zc + cw_ref[0, 1:2, cols] * zbuf[halo - 1:halo - 1 + tm, cols]
            parts.append(zc)
        val, gate = parts
        act[:, j * cc:(j + 1) * cc] = (gate * _sigmoid(gate) * val).astype(BF16)
    zbuf[halo - 2:halo, :] = zbuf[halo + tm - 2:halo + tm, :]
    x2 = x1 + g2 * _dot(act[...], wdn_ref[0])
    if final:
        x2 = _rms(x2, fg_ref[...])
    o_ref[0] = x2


def _post_call(x, oa, ob, oc, mod, w, final_g, l, final):
    b, s, d = x.shape
    tm = TM_POST
    tspec = lambda n: pl.BlockSpec((1, tm, n), lambda i, j: (i, j, 0))
    consts = [w["wo"], w["ng2"], w["wup"], w["cw"], w["cb"], w["wdn"]]
    return pl.pallas_call(
        functools.partial(_post_kernel, final=final),
        grid=(b, s // tm),
        in_specs=[tspec(d), tspec(D_MLA), tspec(D_SG), tspec(D_GLA), _mod_spec(d, l)]
                 + [_layer_spec(a.shape, l) for a in consts] + [_const_spec(final_g.shape)],
        out_specs=tspec(d),
        out_shape=jax.ShapeDtypeStruct((b, s, d), F32),
        scratch_shapes=[pltpu.VMEM((8 + tm, 2 * D_FF), F32), pltpu.VMEM((tm, D_FF), BF16)],
        compiler_params=pltpu.CompilerParams(
            dimension_semantics=("parallel", "arbitrary"), vmem_limit_bytes=VMEM_LIMIT),
        name="post_ffn",
    )(x, oa, ob, oc, mod, *consts, final_g)


def _rot_half_cols(w):
    half = MLA_ROPE // 2
    return jnp.concatenate([-w[..., half:], w[..., :half]], axis=-1)


def _stacked_weights(p):
    depth, d, _ = p["w_in"].shape
    splits = (MLA_Q_LORA, MLA_KV_LORA, MLA_ROPE, D_SG, D_SG, D_GQK, D_GQK, D_GLA, GLA_GATE_RANK, D_GLA)
    offs = [0]
    for n in splits:
        offs.append(offs[-1] + n)
    w_in = p["w_in"].astype(BF16)
    col = lambda i: w_in[..., offs[i]:offs[i + 1]]
    zeros = lambda n: jnp.zeros((depth, d, n), BF16)
    rope_pad = LANES - ROPE_LO - MLA_ROPE
    win = jnp.concatenate([
        col(0), col(1),
        zeros(ROPE_LO), col(2), zeros(rope_pad),
        zeros(ROPE_LO), _rot_half_cols(col(2)), zeros(rope_pad),
        col(3), col(4), col(5), col(6), col(7),
        col(8), zeros(LANES - GLA_GATE_RANK), col(9)], axis=-1)
    assert win.shape[-1] == Z_COLS

    keep3 = ((0, 0), (0, 0), (0, 0))
    wq = p["mla_w_uq"].astype(BF16).reshape(depth, MLA_Q_LORA, MLA_HEADS, MLA_NOPE + MLA_ROPE)
    wuq = jnp.pad(wq, keep3 + ((0, rope_pad),)).reshape(depth, MLA_Q_LORA, MLA_HEADS * HEAD_SLOT)
    wuq = jnp.swapaxes(wuq, 1, 2)
    wuqr = _rot_half_cols(wq[..., MLA_NOPE:]).reshape(depth, MLA_Q_LORA, MLA_HEADS * MLA_ROPE)
    wuqr = jnp.swapaxes(wuqr, 1, 2)
    wkv = p["mla_w_ukv"].astype(BF16).reshape(depth, MLA_KV_LORA, MLA_HEADS, MLA_NOPE + MLA_V)
    wuk = jnp.pad(wkv[..., :MLA_NOPE], keep3 + ((0, HEAD_SLOT - MLA_NOPE),))
    wuk = wuk.reshape(depth, MLA_KV_LORA, MLA_HEADS * HEAD_SLOT)
    wuv = jnp.pad(wkv[..., MLA_NOPE:], keep3 + ((0, VT_ROWS - MLA_V),))
    wuv = jnp.swapaxes(wuv.reshape(depth, MLA_KV_LORA, MLA_HEADS * VT_ROWS), 1, 2)

    bs = jnp.repeat(jnp.swapaxes(p["sg_b_s"], 1, 2), SG_HEAD_DIM, axis=2)
    wg2 = jnp.pad(p["gla_w_gate2"].astype(BF16), ((0, 0), (0, LANES - GLA_GATE_RANK), (0, 0)))
    row = lambda a: a.reshape(depth, 1, -1)
    return {
        "ng1": row(p["norm_mix_g"]), "win": win,
        "qg": row(p["mla_q_norm_g"]), "kvg": row(p["mla_kv_norm_g"]),
        "wuq": wuq, "wuqr": wuqr, "wuk": wuk, "wuv": wuv,
        "lng": row(p["sg_ln_g"]), "lnb": row(p["sg_ln_b"]), "ws": p["sg_w_s"], "bs": bs,
        "wg2": wg2, "bg": row(p["gla_b_gate"]),
        "gng": row(jnp.tile(p["gla_norm_g"], (1, GLA_HEADS))),
        "wo": p["w_out"].astype(BF16), "ng2": row(p["norm_ffn_g"]),
        "wup": p["ffn_w_up"].astype(BF16), "cw": p["ffn_conv_w"],
        "cb": row(p["ffn_conv_b"]), "wdn": p["ffn_w_down"].astype(BF16),
    }


def kernel(x, c, positions, mod_w, mod_b, norm_mix_g, norm_ffn_g, w_in, mla_q_norm_g, mla_kv_norm_g, mla_w_uq, mla_w_ukv, sg_ln_g, sg_ln_b, sg_w_s, sg_b_s, gla_w_gate2, gla_b_gate, gla_norm_g, w_out, ffn_w_up, ffn_conv_w, ffn_conv_b, ffn_w_down, final_norm_g):
    p = dict(norm_mix_g=norm_mix_g, norm_ffn_g=norm_ffn_g, w_in=w_in, mla_q_norm_g=mla_q_norm_g,
             mla_kv_norm_g=mla_kv_norm_g, mla_w_uq=mla_w_uq, mla_w_ukv=mla_w_ukv, sg_ln_g=sg_ln_g,
             sg_ln_b=sg_ln_b, sg_w_s=sg_w_s, sg_b_s=sg_b_s, gla_w_gate2=gla_w_gate2,
             gla_b_gate=gla_b_gate, gla_norm_g=gla_norm_g, w_out=w_out, ffn_w_up=ffn_w_up,
             ffn_conv_w=ffn_conv_w, ffn_conv_b=ffn_conv_b, ffn_w_down=ffn_w_down)
    depth = mod_w.shape[0]
    b = x.shape[0]
    mod = _modulation(c, mod_w, mod_b).reshape(depth, b, 6, D_MODEL)
    tables = _rope_tables(positions)
    fg = final_norm_g.reshape(1, -1)
    w = _stacked_weights(p)
    for l in range(depth):
        q, k, v, o_sg, gq, gk, gla, gv, gr = _pre_call(x, mod, w, tables, l)
        o_mla = _attn_call(q, k, v)
        o_gla = _gla_call(gq, gk, gla, gv, gr, w["gng"], l)
        x = _post_call(x, o_mla, o_sg, o_gla, mod, w, fg, l, final=(l == depth - 1))
    return x
```

```python
import functools

import jax
import jax.numpy as jnp
from jax import lax
from jax.experimental import pallas as pl
from jax.experimental.pallas import tpu as pltpu

D_MODEL = 1024
MLA_HEADS = 8
MLA_NOPE = 64
MLA_ROPE = 32
MLA_V = 64
MLA_Q_LORA = 256
MLA_KV_LORA = 128
ROPE_THETA = 10000.0
SG_HEADS = 4
SG_HEAD_DIM = 64
SG_CHUNK = 128
GLA_HEADS = 4
GLA_DK = 32
GLA_DV = 64
GLA_GATE_RANK = 16
GLA_GATE_TAU = 16.0
GLA_CHUNK = 64
GLA_SUB = 16
D_FF = 2816
CONV_WIDTH = 3
EPS = 1e-6

D_MLA = MLA_HEADS * MLA_V
D_SG = SG_HEADS * SG_HEAD_DIM
D_GLA = GLA_HEADS * GLA_DV
D_GQK = GLA_HEADS * GLA_DK

LANES = 128
HEAD_SLOT = LANES
ROPE_LO = MLA_NOPE
VT_ROWS = MLA_V + 16
VMEM_LIMIT = 56 * 1024 * 1024

Z_CQ = 0
Z_CKV = Z_CQ + MLA_Q_LORA
Z_MISC = Z_CKV + MLA_KV_LORA
Z_SU = Z_MISC + LANES
Z_SV = Z_SU + D_SG
Z_GQ = Z_SV + D_SG
Z_GK = Z_GQ + D_GQK
Z_GV = Z_GK + D_GQK
Z_GR = Z_GV + D_GLA
Z_COLS = Z_GR + D_GLA
MISC_KR = GLA_GATE_RANK
MISC_KRR = MISC_KR + MLA_ROPE
ODD_NOPE_LO = HEAD_SLOT - MLA_NOPE

TM_PRE = 512
TQ_ATTN = 512
TG_GLA = 512
ATTN_LOOKAHEAD = 2
TM_POST = 512
FFN_CHUNK = 256

F32 = jnp.float32
BF16 = jnp.bfloat16
NEG_BIG = -1e30
LOG2_E = 1.4426950408889634


def _dot(a, b):
    return jnp.dot(a, b, preferred_element_type=F32)


def _dot_nt(a, b):
    return lax.dot_general(a, b, (((1,), (1,)), ((), ())), preferred_element_type=F32)


def _dot_tn(a, b):
    return lax.dot_general(a, b, (((0,), (0,)), ((), ())), preferred_element_type=F32)


def _sigmoid(x):
    return 1.0 / (1.0 + jnp.exp(-x))


def _gelu_tanh(x):
    return 0.5 * x * (1.0 + jnp.tanh(0.7978845608028654 * (x + 0.044715 * (x * x * x))))


def _log_sigmoid(x):
    return jnp.minimum(x, 0.0) - jnp.log1p(jnp.exp(-jnp.abs(x)))


def _rms(x, g):
    return x * lax.rsqrt(jnp.mean(x * x, axis=-1, keepdims=True) + EPS) * g


def _const_spec(shape):
    zeros = (0,) * len(shape)
    return pl.BlockSpec(shape, lambda *_: zeros, pipeline_mode=pl.Buffered(1))


def _layer_spec(shape, l):
    idx = (l,) + (0,) * (len(shape) - 1)
    return pl.BlockSpec((1,) + tuple(shape[1:]), lambda *_: idx, pipeline_mode=pl.Buffered(1))


def _mod_spec(d, l):
    return pl.BlockSpec((1, 1, 6, d), lambda i, j: (l, i, 0, 0))


def _mod_kernel(c_ref, w_ref, b_ref, o_ref):
    c = c_ref[...]
    ca = c * _sigmoid(c)
    o_ref[0] = jnp.dot(ca, w_ref[0], preferred_element_type=F32,
                       precision=lax.Precision.HIGHEST) + b_ref[0]


def _modulation(c, mod_w, mod_b):
    depth, d, n = mod_w.shape
    b = c.shape[0]
    tn = 2048
    return pl.pallas_call(
        _mod_kernel,
        grid=(depth, n // tn),
        in_specs=[
            pl.BlockSpec((b, d), lambda l, j: (0, 0)),
            pl.BlockSpec((1, d, tn), lambda l, j: (l, 0, j)),
            pl.BlockSpec((1, 1, tn), lambda l, j: (l, 0, j)),
        ],
        out_specs=pl.BlockSpec((1, b, tn), lambda l, j: (l, 0, j)),
        out_shape=jax.ShapeDtypeStruct((depth, b, n), F32),
        compiler_params=pltpu.CompilerParams(vmem_limit_bytes=VMEM_LIMIT),
        name="modulation",
    )(c, mod_w, mod_b.reshape(depth, 1, n))


def _rope_kernel(pos_ref, post_ref, invf_ref, invft_ref, cos_ref, sin_ref, cost_ref, sint_ref):
    ang = pos_ref[0] * invf_ref[...]
    lane = lax.broadcasted_iota(jnp.int32, ang.shape, 1)
    cos_ref[0] = jnp.where((lane >= MISC_KR) & (lane < MISC_KR + MLA_ROPE), jnp.cos(ang), 0.0)
    sin_ref[0] = jnp.where((lane >= MISC_KRR) & (lane < MISC_KRR + MLA_ROPE), jnp.sin(ang), 0.0)
    angt = invft_ref[...] * post_ref[0]
    cost_ref[0] = jnp.cos(angt)
    sint_ref[0] = jnp.sin(angt)


def _rope_tables(positions):
    b, s = positions.shape
    tm = 512
    inv_freq = ROPE_THETA ** (-jnp.arange(0, MLA_ROPE, 2, dtype=F32) / MLA_ROPE)
    inv2 = jnp.concatenate([inv_freq, inv_freq])
    invf = jnp.zeros((1, LANES), F32).at[0, MISC_KR:MISC_KR + MLA_ROPE].set(inv2)
    invf = invf.at[0, MISC_KRR:MISC_KRR + MLA_ROPE].set(inv2)
    posf = positions.astype(F32)
    out = jax.ShapeDtypeStruct((b, s, LANES), F32)
    outt = jax.ShapeDtypeStruct((b, MLA_ROPE, s), F32)
    return pl.pallas_call(
        _rope_kernel,
        grid=(b, s // tm),
        in_specs=[pl.BlockSpec((1, tm, 1), lambda i, j: (i, j, 0)),
                  pl.BlockSpec((1, 1, tm), lambda i, j: (i, 0, j)),
                  pl.BlockSpec((1, LANES), lambda i, j: (0, 0)),
                  pl.BlockSpec((MLA_ROPE, 1), lambda i, j: (0, 0))],
        out_specs=[pl.BlockSpec((1, tm, LANES), lambda i, j: (i, j, 0))] * 2
                  + [pl.BlockSpec((1, MLA_ROPE, tm), lambda i, j: (i, 0, j))] * 2,
        out_shape=[out, out, outt, outt],
        name="rope_tables",
    )(posf.reshape(b, s, 1), posf.reshape(b, 1, s), invf, inv2.reshape(MLA_ROPE, 1))


def _pre_kernel(x_ref, mod_ref, ng_ref, win_ref, qg_ref, kvg_ref, wuq_ref, wuqr_ref, wuk_ref,
                wuv_ref, cos_ref, sin_ref, cost_ref, sint_ref, lng_ref, lnb_ref, ws_ref, bs_ref,
                wg2_ref, bg_ref,
                q_out, k_out, v_out, sg_out, gq_out, gk_out, gla_out, gv_out, gr_out):
    tm = x_ref.shape[1]
    x = x_ref[0]
    sh1 = mod_ref[0, 0, 0:1, :]
    sc1 = mod_ref[0, 0, 1:2, :]
    h = _rms(x, ng_ref[0]) * (1.0 + sc1) + sh1
    z = _dot(h.astype(BF16), win_ref[0])

    cq = _rms(z[:, Z_CQ:Z_CQ + MLA_Q_LORA], qg_ref[0]).astype(BF16)
    ckv = _rms(z[:, Z_CKV:Z_CKV + MLA_KV_LORA], kvg_ref[0]).astype(BF16)
    qat = _dot_nt(wuq_ref[0], cq)
    qrt = _dot_nt(wuqr_ref[0], cq)
    cost = cost_ref[0]
    sint = sint_ref[0]
    scale = float(MLA_NOPE + MLA_ROPE) ** -0.5 * LOG2_E
    zero_rows = jnp.zeros((HEAD_SLOT - MLA_NOPE - MLA_ROPE, tm), BF16)
    for hd in range(MLA_HEADS):
        r0 = hd * HEAD_SLOT
        nope0, rope0 = (r0, r0 + ROPE_LO) if hd % 2 == 0 else (r0 + ODD_NOPE_LO, r0)
        rot = qrt[hd * MLA_ROPE:(hd + 1) * MLA_ROPE]
        q_out[0, nope0:nope0 + MLA_NOPE, :] = (qat[nope0:nope0 + MLA_NOPE] * scale).astype(BF16)
        q_out[0, rope0:rope0 + MLA_ROPE, :] = (
            (qat[rope0:rope0 + MLA_ROPE] * cost + rot * sint) * scale).astype(BF16)
        q_out[0, rope0 + MLA_ROPE:rope0 + MLA_ROPE + zero_rows.shape[0], :] = zero_rows
    misc = z[:, Z_MISC:Z_MISC + LANES]
    kc = misc * cos_ref[0]
    ks = misc * sin_ref[0]
    kr_even = pltpu.roll(kc, ROPE_LO - MISC_KR, 1) + pltpu.roll(ks, ROPE_LO - MISC_KRR, 1)
    kr_odd = pltpu.roll(kc, LANES - MISC_KR, 1) + pltpu.roll(ks, LANES - MISC_KRR, 1)
    kn = _dot(ckv, wuk_ref[0])
    klane = lax.broadcasted_iota(jnp.int32, (tm, LANES), 1)
    for hd in range(MLA_HEADS):
        pair = kn[:, (hd // 2) * LANES:(hd // 2 + 1) * LANES]
        slot = (jnp.where(klane < MLA_NOPE, pair, kr_even) if hd % 2 == 0
                else jnp.where(klane >= ODD_NOPE_LO, pair, kr_odd))
        k_out[0, :, hd * HEAD_SLOT:(hd + 1) * HEAD_SLOT] = slot.astype(BF16)
    vt = _dot_nt(wuv_ref[0], ckv)
    vrow = lax.broadcasted_iota(jnp.int32, vt.shape, 0)
    v_out[0] = jnp.where(vrow % VT_ROWS == MLA_V, 1.0, vt).astype(BF16)

    u = _gelu_tanh(z[:, Z_SU:Z_SU + D_SG])
    gv = _gelu_tanh(z[:, Z_SV:Z_SV + D_SG])
    mu = jnp.mean(gv, axis=-1, keepdims=True)
    dv = gv - mu
    var = jnp.mean(dv * dv, axis=-1, keepdims=True)
    vn = (dv * lax.rsqrt(var + EPS) * lng_ref[0] + lnb_ref[0]).astype(BF16)
    trow = lax.broadcasted_iota(jnp.int32, (SG_CHUNK, SG_CHUNK), 0)
    tcol = lax.broadcasted_iota(jnp.int32, (SG_CHUNK, SG_CHUNK), 1)
    wmix = [jnp.where(trow >= tcol, ws_ref[0, i], 0.0).astype(BF16) for i in range(SG_HEADS)]
    lane = lax.broadcasted_iota(jnp.int32, (SG_CHUNK, LANES), 1)
    lo_half = lane < SG_HEAD_DIM
    for c in range(tm // SG_CHUNK):
        rows = slice(c * SG_CHUNK, (c + 1) * SG_CHUNK)
        for p in range(D_SG // LANES):
            cols = slice(p * LANES, (p + 1) * LANES)
            vp = vn[rows, cols]
            mixed = jnp.where(lo_half, _dot(wmix[2 * p], vp), _dot(wmix[2 * p + 1], vp))
            sg_out[0, rows, cols] = (u[rows, cols] * (mixed + bs_ref[0, :, cols])).astype(BF16)

    gq_out[0] = z[:, Z_GQ:Z_GQ + D_GQK] * (float(GLA_DK) ** -0.5)
    gk_out[0] = z[:, Z_GK:Z_GK + D_GQK]
    gv_out[0] = z[:, Z_GV:Z_GV + D_GLA]
    gate = _dot(misc.astype(BF16), wg2_ref[0]) + bg_ref[0]
    gla_out[0] = _log_sigmoid(gate) * (1.0 / GLA_GATE_TAU)
    gr = z[:, Z_GR:Z_GR + D_GLA]
    gr_out[0] = gr * _sigmoid(gr)


def _pre_call(x, mod, w, tables, l):
    b, s, d = x.shape
    tm = TM_PRE
    tok = lambda n, dt: jax.ShapeDtypeStruct((b, s, n), dt)
    tspec = lambda n: pl.BlockSpec((1, tm, n), lambda i, j: (i, j, 0))
    tspec_t = lambda n: pl.BlockSpec((1, n, tm), lambda i, j: (i, 0, j))
    consts = [w["ng1"], w["win"], w["qg"], w["kvg"], w["wuq"], w["wuqr"], w["wuk"], w["wuv"]]
    consts2 = [w["lng"], w["lnb"], w["ws"], w["bs"], w["wg2"], w["bg"]]
    in_specs = ([tspec(d), _mod_spec(d, l)]
                + [_layer_spec(a.shape, l) for a in consts]
                + [tspec(LANES), tspec(LANES), tspec_t(MLA_ROPE), tspec_t(MLA_ROPE)]
                + [_layer_spec(a.shape, l) for a in consts2])
    outs = [(MLA_HEADS * HEAD_SLOT, BF16), (MLA_HEADS * HEAD_SLOT, BF16), (D_MLA, BF16),
            (D_SG, BF16), (D_GQK, F32), (D_GQK, F32), (D_GQK, F32), (D_GLA, F32), (D_GLA, F32)]
    out_specs = [tspec(n) for n, _ in outs]
    out_shape = [tok(n, dt) for n, dt in outs]
    out_specs[0] = tspec_t(MLA_HEADS * HEAD_SLOT)
    out_shape[0] = jax.ShapeDtypeStruct((b, MLA_HEADS * HEAD_SLOT, s), BF16)
    out_specs[2] = tspec_t(MLA_HEADS * VT_ROWS)
    out_shape[2] = jax.ShapeDtypeStruct((b, MLA_HEADS * VT_ROWS, s), BF16)
    return pl.pallas_call(
        _pre_kernel,
        grid=(b, s // tm),
        in_specs=in_specs,
        out_specs=out_specs,
        out_shape=out_shape,
        compiler_params=pltpu.CompilerParams(
            dimension_semantics=("parallel", "parallel"), vmem_limit_bytes=VMEM_LIMIT),
        name="pre_mix",
    )(x, mod, *consts, *tables, *consts2)


def _attn_kernel(q_ref, k_ref, vt_ref, o_ref, m_s, acc_s, st_s):
    tq = q_ref.shape[2]
    tk = tq
    qi = pl.program_id(1)
    hk = tk // 2
    tri = (lax.broadcasted_iota(jnp.int32, (hk, hk), 0)
           <= lax.broadcasted_iota(jnp.int32, (hk, hk), 1))

    m_s[...] = jnp.full(m_s.shape, NEG_BIG, F32)
    acc_s[...] = jnp.zeros(acc_s.shape, F32)

    def run_heads(scores, softmax_pv):
        for hd in range(ATTN_LOOKAHEAD):
            scores(hd)
        for hd in range(MLA_HEADS):
            if hd + ATTN_LOOKAHEAD < MLA_HEADS:
                scores(hd + ATTN_LOOKAHEAD)
            softmax_pv(hd)

    def full_step(j):
        off = pl.multiple_of(j * tk, tk)

        def scores(hd):
            hcols = slice(hd * HEAD_SLOT, (hd + 1) * HEAD_SLOT)
            st_s[hd] = _dot(k_ref[0, pl.ds(off, tk), hcols], q_ref[0, hcols, :])

        def softmax_pv(hd):
            m_old = m_s[hd]
            m_new = jnp.maximum(m_old, jnp.max(st_s[hd], axis=0, keepdims=True))
            alpha = jnp.exp2(m_old - m_new)
            m_s[hd] = m_new
            e = jnp.exp2((st_s[hd] - m_new).astype(BF16))
            vrows = slice(hd * VT_ROWS, (hd + 1) * VT_ROWS)
            pv = _dot(vt_ref[0, vrows, pl.ds(off, tk)], e)
            acc_s[vrows, :] = alpha * acc_s[vrows, :] + pv

        run_heads(scores, softmax_pv)

    def diag_step():
        off_a = pl.multiple_of(qi * tk, tk)
        off_b = pl.multiple_of(qi * tk + hk, hk)

        def scores(hd):
            hcols = slice(hd * HEAD_SLOT, (hd + 1) * HEAD_SLOT)
            sa = _dot(k_ref[0, pl.ds(off_a, hk), hcols], q_ref[0, hcols, :])
            sb = _dot(k_ref[0, pl.ds(off_b, hk), hcols], q_ref[0, hcols, hk:])
            st_s[hd, 0:hk, 0:hk] = jnp.where(tri, sa[:, 0:hk], NEG_BIG)
            st_s[hd, 0:hk, hk:] = sa[:, hk:]
            st_s[hd, hk:, hk:] = jnp.where(tri, sb, NEG_BIG)

        def softmax_pv(hd):
            m_old = m_s[hd]
            ma = jnp.max(st_s[hd, 0:hk, :], axis=0, keepdims=True)
            mb = jnp.max(st_s[hd, hk:, hk:], axis=0, keepdims=True)
            m_new = jnp.maximum(m_old, jnp.concatenate(
                [ma[:, 0:hk], jnp.maximum(ma[:, hk:], mb)], axis=1))
            alpha = jnp.exp2(m_old - m_new)
            m_s[hd] = m_new
            ea = jnp.exp2((st_s[hd, 0:hk, :] - m_new).astype(BF16))
            eb = jnp.exp2((st_s[hd, hk:, hk:] - m_new[:, hk:]).astype(BF16))
            vrows = slice(hd * VT_ROWS, (hd + 1) * VT_ROWS)
            pva = _dot(vt_ref[0, vrows, pl.ds(off_a, hk)], ea)
            pvb = _dot(vt_ref[0, vrows, pl.ds(off_b, hk)], eb)
            acc_s[vrows, 0:hk] = alpha[:, 0:hk] * acc_s[vrows, 0:hk] + pva[:, 0:hk]
            acc_s[vrows, hk:] = alpha[:, hk:] * acc_s[vrows, hk:] + pva[:, hk:] + pvb

        run_heads(scores, softmax_pv)

    def body(j, carry):
        full_step(j)
        return carry

    lax.fori_loop(0, qi, body, 0)
    diag_step()

    for p in range(MLA_HEADS // 2):
        halves = []
        for hd in (2 * p, 2 * p + 1):
            r0 = hd * VT_ROWS
            halves.append(acc_s[r0:r0 + MLA_V, :] * (1.0 / acc_s[r0 + MLA_V:r0 + MLA_V + 1, :]))
        ot = jnp.concatenate(halves, axis=0)
        o_ref[0, :, p * LANES:(p + 1) * LANES] = ot.T.astype(BF16)


def _attn_call(qt, k, vt):
    b, s, _ = k.shape
    tq = TQ_ATTN
    return pl.pallas_call(
        _attn_kernel,
        grid=(b, s // tq),
        in_specs=[pl.BlockSpec((1, MLA_HEADS * HEAD_SLOT, tq), lambda i, j: (i, 0, j)),
                  pl.BlockSpec((1, s, MLA_HEADS * HEAD_SLOT), lambda i, j: (i, 0, 0)),
                  pl.BlockSpec((1, MLA_HEADS * VT_ROWS, s), lambda i, j: (i, 0, 0))],
        out_specs=pl.BlockSpec((1, tq, D_MLA), lambda i, j: (i, j, 0)),
        out_shape=jax.ShapeDtypeStruct((b, s, D_MLA), BF16),
        scratch_shapes=[pltpu.VMEM((MLA_HEADS, 1, tq), F32),
                        pltpu.VMEM((MLA_HEADS * VT_ROWS, tq), F32),
                        pltpu.VMEM((MLA_HEADS, tq, tq), F32)],
        compiler_params=pltpu.CompilerParams(
            dimension_semantics=("parallel", "arbitrary"), vmem_limit_bytes=VMEM_LIMIT),
        name="mla_attention",
    )(qt, k, vt)


def _gla_kernel(q_ref, k_ref, la_ref, v_ref, r_ref, g_ref, o_ref, st_ref, kpad, bpad, vpad):
    tg = q_ref.shape[1]
    ch = GLA_CHUNK
    sb = GLA_SUB

    @pl.when(pl.program_id(1) == 0)
    def _():
        st_ref[...] = jnp.zeros_like(st_ref)
        kpad[0:sb, :] = jnp.zeros((sb, D_GQK), F32)
        bpad[0:sb, :] = jnp.zeros((sb, D_GQK), F32)
        vpad[0:sb, :] = jnp.zeros((sb, D_GLA), F32)

    q = q_ref[0]
    k = k_ref[0]
    v = v_ref[0]
    ti = lax.broadcasted_iota(jnp.int32, (ch, ch), 0)
    tj = lax.broadcasted_iota(jnp.int32, (ch, ch), 1)
    tri = jnp.where(tj <= ti, 1.0, 0.0).astype(BF16)
    la = la_ref[0] * LOG2_E
    la_hi = la.astype(BF16)
    la_r1 = la - la_hi.astype(F32)
    la_mid = la_r1.astype(BF16)
    la_lo = (la_r1 - la_mid.astype(F32)).astype(BF16)
    b = jnp.concatenate(
        [_dot(tri, la_hi[c * ch:(c + 1) * ch]) + _dot(tri, la_mid[c * ch:(c + 1) * ch])
         + _dot(tri, la_lo[c * ch:(c + 1) * ch]) for c in range(tg // ch)], axis=0)
    kpad[sb:sb + tg, :] = k
    bpad[sb:sb + tg, :] = b
    vpad[sb:sb + tg, :] = v

    er = lax.broadcasted_iota(jnp.int32, (D_GQK, D_GLA), 0)
    ec = lax.broadcasted_iota(jnp.int32, (D_GQK, D_GLA), 1)
    head_sum = jnp.where((er // GLA_DK) == (ec // GLA_DV), 1.0, 0.0).astype(BF16)

    rowmod = lax.broadcasted_iota(jnp.int32, (tg, D_GQK), 0) % sb
    acc = _dot((q * k).astype(BF16), head_sum) * v
    for s in range(1, sb):
        ks = kpad[sb - s:sb - s + tg, :]
        bs = bpad[sb - s:sb - s + tg, :]
        vs = vpad[sb - s:sb - s + tg, :]
        p = jnp.where(rowmod >= s, q * ks * jnp.exp2(b - bs), 0.0)
        acc = acc + _dot(p.astype(BF16), head_sum) * vs

    sr = lax.broadcasted_iota(jnp.int32, (D_GLA, D_GQK), 0)
    sc = lax.broadcasted_iota(jnp.int32, (D_GLA, D_GQK), 1)
    same_head = (sr // GLA_DV) == (sc // GLA_DK)
    gr = lax.broadcasted_iota(jnp.int32, (D_GLA, D_GLA), 0)
    gc = lax.broadcasted_iota(jnp.int32, (D_GLA, D_GLA), 1)
    head_mean = jnp.where((gr // GLA_DV) == (gc // GLA_DV), 1.0 / GLA_DV, 0.0).astype(BF16)
    klane = lax.broadcasted_iota(jnp.int32, (sb, D_GQK), 1) // GLA_DK
    vlane = lax.broadcasted_iota(jnp.int32, (sb, D_GLA), 1) // GLA_DV
    krow = lax.broadcasted_iota(jnp.int32, (ch, D_GQK), 0)
    nch = tg // ch
    chunk = lambda a, c: a[c * ch:(c + 1) * ch]
    vb = v.astype(BF16)

    scores = []
    for c in range(nch):
        bc, qc, kc = chunk(b, c), chunk(q, c), chunk(k, c)
        for blk in range(1, ch // sb):
            lo_r = blk * sb
            r = bc[lo_r:lo_r + 1, :]
            qt = qc[lo_r:lo_r + sb] * jnp.exp2(bc[lo_r:lo_r + sb] - r)
            kt = (kc * jnp.exp2(jnp.where(krow < lo_r, r - bc, NEG_BIG))).astype(BF16)
            qexp = jnp.concatenate([jnp.where(klane == hd, qt, 0.0) for hd in range(GLA_HEADS)], axis=0)
            scores.append(_dot_nt(qexp.astype(BF16), kt))
    upds = []
    for c in range(nch):
        bc = chunk(b, c)
        kd = (chunk(k, c) * jnp.exp2(bc[ch - 1:ch, :] - bc)).astype(BF16)
        upds.append(jnp.where(same_head, _dot_tn(chunk(vb, c), kd), 0.0))
    o_sub = []
    for c in range(nch):
        pieces = [jnp.zeros((sb, D_GLA), F32)]
        for blk in range(1, ch // sb):
            oh = _dot(scores[c * (ch // sb - 1) + blk - 1].astype(BF16), chunk(vb, c))
            piece = jnp.where(vlane == 0, oh[0:sb], 0.0)
            for hd in range(1, GLA_HEADS):
                piece = piece + jnp.where(vlane == hd, oh[hd * sb:(hd + 1) * sb], 0.0)
            pieces.append(piece)
        o_sub.append(jnp.concatenate(pieces, axis=0))
    states = []
    st = st_ref[...]
    for c in range(nch):
        states.append(st.astype(BF16))
        st = st * jnp.exp2(chunk(b, c)[ch - 1:ch, :]) + upds[c]
    st_ref[...] = st
    o = jnp.concatenate(
        [chunk(acc, c) + o_sub[c]
         + _dot_nt((chunk(q, c) * jnp.exp2(chunk(b, c))).astype(BF16), states[c]) for c in range(nch)], axis=0)
    o2 = o * o
    hi = o2.astype(BF16)
    lo = (o2 - hi.astype(F32)).astype(BF16)
    ms = _dot(hi, head_mean) + _dot(lo, head_mean)
    o_ref[0] = (o * lax.rsqrt(ms + EPS) * g_ref[0] * r_ref[0]).astype(BF16)


def _gla_call(gq, gk, gla, gv, gr, g256, l):
    b, s, _ = gq.shape
    tg = TG_GLA
    tspec = lambda n: pl.BlockSpec((1, tg, n), lambda i, j: (i, j, 0))
    return pl.pallas_call(
        _gla_kernel,
        grid=(b, s // tg),
        in_specs=[tspec(D_GQK), tspec(D_GQK), tspec(D_GQK), tspec(D_GLA), tspec(D_GLA),
                  _layer_spec(g256.shape, l)],
        out_specs=tspec(D_GLA),
        out_shape=jax.ShapeDtypeStruct((b, s, D_GLA), BF16),
        scratch_shapes=[pltpu.VMEM((D_GLA, D_GQK), F32),
                        pltpu.VMEM((GLA_SUB + tg, D_GQK), F32),
                        pltpu.VMEM((GLA_SUB + tg, D_GQK), F32),
                        pltpu.VMEM((GLA_SUB + tg, D_GLA), F32)],
        compiler_params=pltpu.CompilerParams(
            dimension_semantics=("parallel", "arbitrary"), vmem_limit_bytes=VMEM_LIMIT),
        name="gla",
    )(gq, gk, gla, gv, gr, g256)


def _post_kernel(x_ref, oa_ref, ob_ref, oc_ref, mod_ref, wo_ref, ng_ref, wup_ref, cw_ref, cb_ref,
                 wdn_ref, fg_ref, o_ref, zbuf, act, *, final):
    tm = x_ref.shape[1]
    halo = 8
    g1 = mod_ref[0, 0, 2:3, :]
    sh2 = mod_ref[0, 0, 3:4, :]
    sc2 = mod_ref[0, 0, 4:5, :]
    g2 = mod_ref[0, 0, 5:6, :]
    mix = (_dot(oa_ref[0], wo_ref[0, 0:D_MLA, :])
           + _dot(ob_ref[0], wo_ref[0, D_MLA:D_MLA + D_SG, :])
           + _dot(oc_ref[0], wo_ref[0, D_MLA + D_SG:, :]))
    x1 = x_ref[0] + g1 * mix
    h2 = _rms(x1, ng_ref[0]) * (1.0 + sc2) + sh2

    @pl.when(pl.program_id(1) == 0)
    def _():
        zbuf[0:halo, :] = jnp.zeros((halo, 2 * D_FF), F32)

    zbuf[halo:halo + tm, :] = _dot(h2.astype(BF16), wup_ref[0])
    cc = FFN_CHUNK
    for j in range(D_FF // cc):
        parts = []
        for cols in (slice(j * cc, (j + 1) * cc), slice(D_FF + j * cc, D_FF + (j + 1) * cc)):
            zc = cb_ref[0, :, cols] + cw_ref[0, 2:3, cols] * zbuf[halo:halo + tm, cols]
            zc = zc + cw_ref[0, 0:1, cols] * zbuf[halo - 2:halo - 2 + tm, cols]
            zc = zc + cw_ref[0, 1:2, cols] * zbuf[halo - 1:halo - 1 + tm, cols]
            parts.append(zc)
        val, gate = parts
        act[:, j * cc:(j + 1) * cc] = (gate * _sigmoid(gate) * val).astype(BF16)
    zbuf[halo - 2:halo, :] = zbuf[halo + tm - 2:halo + tm, :]
    x2 = x1 + g2 * _dot(act[...], wdn_ref[0])
    if final:
        x2 = _rms(x2, fg_ref[...])
    o_ref[0] = x2


def _post_call(x, oa, ob, oc, mod, w, final_g, l, final):
    b, s, d = x.shape
    tm = TM_POST
    tspec = lambda n: pl.BlockSpec((1, tm, n), lambda i, j: (i, j, 0))
    consts = [w["wo"], w["ng2"], w["wup"], w["cw"], w["cb"], w["wdn"]]
    return pl.pallas_call(
        functools.partial(_post_kernel, final=final),
        grid=(b, s // tm),
        in_specs=[tspec(d), tspec(D_MLA), tspec(D_SG), tspec(D_GLA), _mod_spec(d, l)]
                 + [_layer_spec(a.shape, l) for a in consts] + [_const_spec(final_g.shape)],
        out_specs=tspec(d),
        out_shape=jax.ShapeDtypeStruct((b, s, d), F32),
        scratch_shapes=[pltpu.VMEM((8 + tm, 2 * D_FF), F32), pltpu.VMEM((tm, D_FF), BF16)],
        compiler_params=pltpu.CompilerParams(
            dimension_semantics=("parallel", "arbitrary"), vmem_limit_bytes=VMEM_LIMIT),
        name="post_ffn",
    )(x, oa, ob, oc, mod, *consts, final_g)


def _rot_half_cols(w):
    half = MLA_ROPE // 2
    return jnp.concatenate([-w[..., half:], w[..., :half]], axis=-1)


def _stacked_weights(p):
    depth, d, _ = p["w_in"].shape
    splits = (MLA_Q_LORA, MLA_KV_LORA, MLA_ROPE, D_SG, D_SG, D_GQK, D_GQK, D_GLA, GLA_GATE_RANK, D_GLA)
    offs = [0]
    for n in splits:
        offs.append(offs[-1] + n)
    w_in = p["w_in"]
    col = lambda i: w_in[..., offs[i]:offs[i + 1]]
    win = jnp.concatenate([
        w_in[..., :offs[2]],
        col(8), col(2), _rot_half_cols(col(2)), jnp.zeros((depth, d, LANES - MISC_KRR - MLA_ROPE), F32),
        w_in[..., offs[3]:offs[8]], col(9)], axis=-1).astype(BF16)
    assert win.shape[-1] == Z_COLS

    keep3 = ((0, 0), (0, 0), (0, 0))
    wq = p["mla_w_uq"].astype(BF16).reshape(depth, MLA_Q_LORA, MLA_HEADS, MLA_NOPE + MLA_ROPE)
    nope, rope = wq[..., :MLA_NOPE], wq[..., MLA_NOPE:]
    pad_cols = jnp.zeros(rope.shape[:-1] + (HEAD_SLOT - MLA_NOPE - MLA_ROPE,), BF16)
    odd_head = (jnp.arange(MLA_HEADS) % 2 == 1)[None, None, :, None]
    wuq = jnp.where(odd_head, jnp.concatenate([rope, pad_cols, nope], axis=-1),
                    jnp.concatenate([nope, rope, pad_cols], axis=-1))
    wuq = jnp.swapaxes(wuq.reshape(depth, MLA_Q_LORA, MLA_HEADS * HEAD_SLOT), 1, 2)
    wuqr = _rot_half_cols(rope).reshape(depth, MLA_Q_LORA, MLA_HEADS * MLA_ROPE)
    wuqr = jnp.swapaxes(wuqr, 1, 2)
    wkv = p["mla_w_ukv"].astype(BF16).reshape(depth, MLA_KV_LORA, MLA_HEADS, MLA_NOPE + MLA_V)
    wuk = wkv[..., :MLA_NOPE].reshape(depth, MLA_KV_LORA, MLA_HEADS * MLA_NOPE)
    wuv = jnp.pad(wkv[..., MLA_NOPE:], keep3 + ((0, VT_ROWS - MLA_V),))
    wuv = jnp.swapaxes(wuv.reshape(depth, MLA_KV_LORA, MLA_HEADS * VT_ROWS), 1, 2)

    bs = jnp.repeat(jnp.swapaxes(p["sg_b_s"], 1, 2), SG_HEAD_DIM, axis=2)
    wg2 = jnp.pad(p["gla_w_gate2"].astype(BF16), ((0, 0), (0, LANES - GLA_GATE_RANK), (0, 0)))
    row = lambda a: a.reshape(depth, 1, -1)
    return {
        "ng1": row(p["norm_mix_g"]), "win": win,
        "qg": row(p["mla_q_norm_g"]), "kvg": row(p["mla_kv_norm_g"]),
        "wuq": wuq, "wuqr": wuqr, "wuk": wuk, "wuv": wuv,
        "lng": row(p["sg_ln_g"]), "lnb": row(p["sg_ln_b"]), "ws": p["sg_w_s"], "bs": bs,
        "wg2": wg2, "bg": row(p["gla_b_gate"]),
        "gng": row(jnp.tile(p["gla_norm_g"], (1, GLA_HEADS))),
        "wo": p["w_out"].astype(BF16), "ng2": row(p["norm_ffn_g"]),
        "wup": p["ffn_w_up"].astype(BF16), "cw": p["ffn_conv_w"],
        "cb": row(p["ffn_conv_b"]), "wdn": p["ffn_w_down"].astype(BF16),
    }


def kernel(x, c, positions, mod_w, mod_b, norm_mix_g, norm_ffn_g, w_in, mla_q_norm_g, mla_kv_norm_g, mla_w_uq, mla_w_ukv, sg_ln_g, sg_ln_b, sg_w_s, sg_b_s, gla_w_gate2, gla_b_gate, gla_norm_g, w_out, ffn_w_up, ffn_conv_w, ffn_conv_b, ffn_w_down, final_norm_g):
    p = dict(norm_mix_g=norm_mix_g, norm_ffn_g=norm_ffn_g, w_in=w_in, mla_q_norm_g=mla_q_norm_g,
             mla_kv_norm_g=mla_kv_norm_g, mla_w_uq=mla_w_uq, mla_w_ukv=mla_w_ukv, sg_ln_g=sg_ln_g,
             sg_ln_b=sg_ln_b, sg_w_s=sg_w_s, sg_b_s=sg_b_s, gla_w_gate2=gla_w_gate2,
             gla_b_gate=gla_b_gate, gla_norm_g=gla_norm_g, w_out=w_out, ffn_w_up=ffn_w_up,
             ffn_conv_w=ffn_conv_w, ffn_conv_b=ffn_conv_b, ffn_w_down=ffn_w_down)
    depth = mod_w.shape[0]
    b = x.shape[0]
    mod = _modulation(c, mod_w, mod_b).reshape(depth, b, 6, D_MODEL)
    tables = _rope_tables(positions)
    fg = final_norm_g.reshape(1, -1)
    w = _stacked_weights(p)
    for l in range(depth):
        q, k, v, o_sg, gq, gk, gla, gv, gr = _pre_call(x, mod, w, tables, l)
        o_mla = _attn_call(q, k, v)
        o_gla = _gla_call(gq, gk, gla, gv, gr, w["gng"], l)
        x = _post_call(x, o_mla, o_sg, o_gla, mod, w, fg, l, final=(l == depth - 1))
    return x
```

```python
import functools

import jax
import jax.numpy as jnp
from jax import lax
from jax.experimental import pallas as pl
from jax.experimental.pallas import tpu as pltpu

D_MODEL = 1024
MLA_HEADS = 8
MLA_NOPE = 64
MLA_ROPE = 32
MLA_V = 64
MLA_Q_LORA = 256
MLA_KV_LORA = 128
ROPE_THETA = 10000.0
SG_HEADS = 4
SG_HEAD_DIM = 64
SG_CHUNK = 128
GLA_HEADS = 4
GLA_DK = 32
GLA_DV = 64
GLA_GATE_RANK = 16
GLA_GATE_TAU = 16.0
GLA_CHUNK = 64
GLA_SUB = 8
D_FF = 2816
CONV_WIDTH = 3
EPS = 1e-6

D_MLA = MLA_HEADS * MLA_V
D_SG = SG_HEADS * SG_HEAD_DIM
D_GLA = GLA_HEADS * GLA_DV
D_GQK = GLA_HEADS * GLA_DK

LANES = 128
HEAD_SLOT = LANES
ROPE_LO = MLA_NOPE
VT_ROWS = MLA_V + 16
VMEM_LIMIT = 56 * 1024 * 1024

Z_CQ = 0
Z_CKV = Z_CQ + MLA_Q_LORA
Z_MISC = Z_CKV + MLA_KV_LORA
Z_SU = Z_MISC + LANES
Z_SV = Z_SU + D_SG
Z_GQ = Z_SV + D_SG
Z_GK = Z_GQ + D_GQK
Z_GV = Z_GK + D_GQK
Z_GR = Z_GV + D_GLA
Z_COLS = Z_GR + D_GLA
MISC_KR = GLA_GATE_RANK
MISC_KRR = MISC_KR + MLA_ROPE
ODD_NOPE_LO = HEAD_SLOT - MLA_NOPE

TM_PRE = 512
TQ_ATTN = 512
TG_GLA = 512
ATTN_LOOKAHEAD = 2
TM_POST = 512
FFN_CHUNK = 256

F32 = jnp.float32
BF16 = jnp.bfloat16
NEG_BIG = -1e30
LOG2_E = 1.4426950408889634


def _dot(a, b):
    return jnp.dot(a, b, preferred_element_type=F32)


def _dot_nt(a, b):
    return lax.dot_general(a, b, (((1,), (1,)), ((), ())), preferred_element_type=F32)


def _dot_tn(a, b):
    return lax.dot_general(a, b, (((0,), (0,)), ((), ())), preferred_element_type=F32)


def _sigmoid(x):
    return 1.0 / (1.0 + jnp.exp(-x))


def _gelu_tanh(x):
    return 0.5 * x * (1.0 + jnp.tanh(0.7978845608028654 * (x + 0.044715 * (x * x * x))))


def _log_sigmoid(x):
    return jnp.minimum(x, 0.0) - jnp.log1p(jnp.exp(-jnp.abs(x)))


def _rms(x, g):
    return x * lax.rsqrt(jnp.mean(x * x, axis=-1, keepdims=True) + EPS) * g


def _const_spec(shape):
    zeros = (0,) * len(shape)
    return pl.BlockSpec(shape, lambda *_: zeros, pipeline_mode=pl.Buffered(1))


def _layer_spec(shape, l):
    idx = (l,) + (0,) * (len(shape) - 1)
    return pl.BlockSpec((1,) + tuple(shape[1:]), lambda *_: idx, pipeline_mode=pl.Buffered(1))


def _mod_spec(d, l):
    return pl.BlockSpec((1, 1, 6, d), lambda i, j: (l, i, 0, 0))


def _mod_kernel(c_ref, w_ref, b_ref, o_ref):
    c = c_ref[...]
    ca = c * _sigmoid(c)
    o_ref[0] = jnp.dot(ca, w_ref[0], preferred_element_type=F32,
                       precision=lax.Precision.HIGHEST) + b_ref[0]


def _modulation(c, mod_w, mod_b):
    depth, d, n = mod_w.shape
    b = c.shape[0]
    tn = 2048
    return pl.pallas_call(
        _mod_kernel,
        grid=(depth, n // tn),
        in_specs=[
            pl.BlockSpec((b, d), lambda l, j: (0, 0)),
            pl.BlockSpec((1, d, tn), lambda l, j: (l, 0, j)),
            pl.BlockSpec((1, 1, tn), lambda l, j: (l, 0, j)),
        ],
        out_specs=pl.BlockSpec((1, b, tn), lambda l, j: (l, 0, j)),
        out_shape=jax.ShapeDtypeStruct((depth, b, n), F32),
        compiler_params=pltpu.CompilerParams(vmem_limit_bytes=VMEM_LIMIT),
        name="modulation",
    )(c, mod_w, mod_b.reshape(depth, 1, n))


def _rope_kernel(post_ref, invft_ref, cos_ref, sin_ref, cost_ref, sint_ref):
    angt = invft_ref[...] * post_ref[0]
    cost = jnp.cos(angt)
    sint = jnp.sin(angt)
    cost_ref[0] = cost
    sint_ref[0] = sint
    tm = angt.shape[1]
    zrows = lambda n: jnp.zeros((n, tm), F32)
    cos_ref[0] = jnp.concatenate([zrows(MISC_KR), cost, zrows(LANES - MISC_KR - MLA_ROPE)], axis=0).T
    sin_ref[0] = jnp.concatenate([zrows(MISC_KRR), sint, zrows(LANES - MISC_KRR - MLA_ROPE)], axis=0).T


def _rope_tables(positions):
    b, s = positions.shape
    tm = 512
    inv_freq = ROPE_THETA ** (-jnp.arange(0, MLA_ROPE, 2, dtype=F32) / MLA_ROPE)
    inv2 = jnp.concatenate([inv_freq, inv_freq])
    posf = positions.astype(F32)
    out = jax.ShapeDtypeStruct((b, s, LANES), F32)
    outt = jax.ShapeDtypeStruct((b, MLA_ROPE, s), F32)
    return pl.pallas_call(
        _rope_kernel,
        grid=(b, s // tm),
        in_specs=[pl.BlockSpec((1, 1, tm), lambda i, j: (i, 0, j)),
                  pl.BlockSpec((MLA_ROPE, 1), lambda i, j: (0, 0))],
        out_specs=[pl.BlockSpec((1, tm, LANES), lambda i, j: (i, j, 0))] * 2
                  + [pl.BlockSpec((1, MLA_ROPE, tm), lambda i, j: (i, 0, j))] * 2,
        out_shape=[out, out, outt, outt],
        name="rope_tables",
    )(posf.reshape(b, 1, s), inv2.reshape(MLA_ROPE, 1))


def _pre_kernel(x_ref, mod_ref, ng_ref, win_ref, qg_ref, kvg_ref, wuq_ref, wuqr_ref, wuk_ref,
                wuv_ref, cos_ref, sin_ref, cost_ref, sint_ref, lng_ref, lnb_ref, ws_ref, bs_ref,
                wg2_ref, bg_ref,
                q_out, k_out, v_out, sg_out, gq_out, gk_out, gla_out, gv_out, gr_out):
    tm = x_ref.shape[1]
    x = x_ref[0]
    sh1 = mod_ref[0, 0, 0:1, :]
    sc1 = mod_ref[0, 0, 1:2, :]
    h = _rms(x, ng_ref[0]) * (1.0 + sc1) + sh1
    z = _dot(h.astype(BF16), win_ref[0])

    cq = _rms(z[:, Z_CQ:Z_CQ + MLA_Q_LORA], qg_ref[0]).astype(BF16)
    ckv = _rms(z[:, Z_CKV:Z_CKV + MLA_KV_LORA], kvg_ref[0]).astype(BF16)
    qat = _dot_nt(wuq_ref[0], cq)
    qrt = _dot_nt(wuqr_ref[0], cq)
    cost = cost_ref[0]
    sint = sint_ref[0]
    scale = float(MLA_NOPE + MLA_ROPE) ** -0.5 * LOG2_E
    zero_rows = jnp.zeros((HEAD_SLOT - MLA_NOPE - MLA_ROPE, tm), BF16)
    for hd in range(MLA_HEADS):
        r0 = hd * HEAD_SLOT
        nope0, rope0 = (r0, r0 + ROPE_LO) if hd % 2 == 0 else (r0 + ODD_NOPE_LO, r0)
        rot = qrt[hd * MLA_ROPE:(hd + 1) * MLA_ROPE]
        q_out[0, nope0:nope0 + MLA_NOPE, :] = (qat[nope0:nope0 + MLA_NOPE] * scale).astype(BF16)
        q_out[0, rope0:rope0 + MLA_ROPE, :] = (
            (qat[rope0:rope0 + MLA_ROPE] * cost + rot * sint) * scale).astype(BF16)
        q_out[0, rope0 + MLA_ROPE:rope0 + MLA_ROPE + zero_rows.shape[0], :] = zero_rows
    misc = z[:, Z_MISC:Z_MISC + LANES]
    kc = misc * cos_ref[0]
    ks = misc * sin_ref[0]
    kr_even = pltpu.roll(kc, ROPE_LO - MISC_KR, 1) + pltpu.roll(ks, ROPE_LO - MISC_KRR, 1)
    kr_odd = pltpu.roll(kc, LANES - MISC_KR, 1) + pltpu.roll(ks, LANES - MISC_KRR, 1)
    kn = _dot(ckv, wuk_ref[0])
    klane = lax.broadcasted_iota(jnp.int32, (tm, LANES), 1)
    for hd in range(MLA_HEADS):
        pair = kn[:, (hd // 2) * LANES:(hd // 2 + 1) * LANES]
        slot = (jnp.where(klane < MLA_NOPE, pair, kr_even) if hd % 2 == 0
                else jnp.where(klane >= ODD_NOPE_LO, pair, kr_odd))
        k_out[0, :, hd * HEAD_SLOT:(hd + 1) * HEAD_SLOT] = slot.astype(BF16)
    vt = _dot_nt(wuv_ref[0], ckv)
    vrow = lax.broadcasted_iota(jnp.int32, vt.shape, 0)
    v_out[0] = jnp.where(vrow % VT_ROWS == MLA_V, 1.0, vt).astype(BF16)

    u = _gelu_tanh(z[:, Z_SU:Z_SU + D_SG])
    gv = _gelu_tanh(z[:, Z_SV:Z_SV + D_SG])
    mu = jnp.mean(gv, axis=-1, keepdims=True)
    dv = gv - mu
    var = jnp.mean(dv * dv, axis=-1, keepdims=True)
    vn = (dv * lax.rsqrt(var + EPS) * lng_ref[0] + lnb_ref[0]).astype(BF16)
    trow = lax.broadcasted_iota(jnp.int32, (SG_CHUNK, SG_CHUNK), 0)
    tcol = lax.broadcasted_iota(jnp.int32, (SG_CHUNK, SG_CHUNK), 1)
    wmix = [jnp.where(trow >= tcol, ws_ref[0, i], 0.0).astype(BF16) for i in range(SG_HEADS)]
    lane = lax.broadcasted_iota(jnp.int32, (SG_CHUNK, LANES), 1)
    lo_half = lane < SG_HEAD_DIM
    nchunk = tm // SG_CHUNK
    for p in range(D_SG // LANES):
        cols = slice(p * LANES, (p + 1) * LANES)
        vp = jnp.concatenate([vn[c * SG_CHUNK:(c + 1) * SG_CHUNK, cols] for c in range(nchunk)], axis=1)
        m_lo = _dot(wmix[2 * p], vp)
        m_hi = _dot(wmix[2 * p + 1], vp)
        for c in range(nchunk):
            rows = slice(c * SG_CHUNK, (c + 1) * SG_CHUNK)
            blk = slice(c * LANES, (c + 1) * LANES)
            mixed = jnp.where(lo_half, m_lo[:, blk], m_hi[:, blk])
            sg_out[0, rows, cols] = (u[rows, cols] * (mixed + bs_ref[0, :, cols])).astype(BF16)

    gq_out[0] = z[:, Z_GQ:Z_GQ + D_GQK] * (float(GLA_DK) ** -0.5)
    gk_out[0] = z[:, Z_GK:Z_GK + D_GQK]
    gv_out[0] = z[:, Z_GV:Z_GV + D_GLA]
    gate = _dot(misc.astype(BF16), wg2_ref[0]) + bg_ref[0]
    gla_out[0] = _log_sigmoid(gate) * (1.0 / GLA_GATE_TAU)
    gr = z[:, Z_GR:Z_GR + D_GLA]
    gr_out[0] = gr * _sigmoid(gr)


def _pre_call(x, mod, w, tables, l):
    b, s, d = x.shape
    tm = TM_PRE
    tok = lambda n, dt: jax.ShapeDtypeStruct((b, s, n), dt)
    tspec = lambda n: pl.BlockSpec((1, tm, n), lambda i, j: (i, j, 0))
    tspec_t = lambda n: pl.BlockSpec((1, n, tm), lambda i, j: (i, 0, j))
    consts = [w["ng1"], w["win"], w["qg"], w["kvg"], w["wuq"], w["wuqr"], w["wuk"], w["wuv"]]
    consts2 = [w["lng"], w["lnb"], w["ws"], w["bs"], w["wg2"], w["bg"]]
    in_specs = ([tspec(d), _mod_spec(d, l)]
                + [_layer_spec(a.shape, l) for a in consts]
                + [tspec(LANES), tspec(LANES), tspec_t(MLA_ROPE), tspec_t(MLA_ROPE)]
                + [_layer_spec(a.shape, l) for a in consts2])
    outs = [(MLA_HEADS * HEAD_SLOT, BF16), (MLA_HEADS * HEAD_SLOT, BF16), (D_MLA, BF16),
            (D_SG, BF16), (D_GQK, F32), (D_GQK, F32), (D_GQK, F32), (D_GLA, F32), (D_GLA, F32)]
    out_specs = [tspec(n) for n, _ in outs]
    out_shape = [tok(n, dt) for n, dt in outs]
    out_specs[0] = tspec_t(MLA_HEADS * HEAD_SLOT)
    out_shape[0] = jax.ShapeDtypeStruct((b, MLA_HEADS * HEAD_SLOT, s), BF16)
    out_specs[2] = tspec_t(MLA_HEADS * VT_ROWS)
    out_shape[2] = jax.ShapeDtypeStruct((b, MLA_HEADS * VT_ROWS, s), BF16)
    return pl.pallas_call(
        _pre_kernel,
        grid=(b, s // tm),
        in_specs=in_specs,
        out_specs=out_specs,
        out_shape=out_shape,
        compiler_params=pltpu.CompilerParams(
            dimension_semantics=("parallel", "parallel"), vmem_limit_bytes=VMEM_LIMIT),
        name="pre_mix",
    )(x, mod, *consts, *tables, *consts2)


def _attn_kernel(q_ref, k_ref, vt_ref, o_ref, m_s, acc_s, st_s):
    tq = q_ref.shape[2]
    tk = tq
    qi = pl.program_id(1)
    hk = tk // 2
    tri = (lax.broadcasted_iota(jnp.int32, (hk, hk), 0)
           <= lax.broadcasted_iota(jnp.int32, (hk, hk), 1))

    m_s[...] = jnp.full(m_s.shape, NEG_BIG, F32)
    acc_s[...] = jnp.zeros(acc_s.shape, F32)

    def run_heads(scores, softmax_pv):
        for hd in range(ATTN_LOOKAHEAD):
            scores(hd)
        for hd in range(MLA_HEADS):
            if hd + ATTN_LOOKAHEAD < MLA_HEADS:
                scores(hd + ATTN_LOOKAHEAD)
            softmax_pv(hd)

    def full_step(j):
        off = pl.multiple_of(j * tk, tk)

        def scores(hd):
            hcols = slice(hd * HEAD_SLOT, (hd + 1) * HEAD_SLOT)
            st_s[hd] = _dot(k_ref[0, pl.ds(off, tk), hcols], q_ref[0, hcols, :])

        def softmax_pv(hd):
            m_old = m_s[hd]
            m_new = jnp.maximum(m_old, jnp.max(st_s[hd], axis=0, keepdims=True))
            alpha = jnp.exp2(m_old - m_new)
            m_s[hd] = m_new
            e = jnp.exp2((st_s[hd] - m_new).astype(BF16))
            vrows = slice(hd * VT_ROWS, (hd + 1) * VT_ROWS)
            pv = _dot(vt_ref[0, vrows, pl.ds(off, tk)], e)
            acc_s[vrows, :] = alpha * acc_s[vrows, :] + pv

        run_heads(scores, softmax_pv)

    def diag_step():
        off_a = pl.multiple_of(qi * tk, tk)
        off_b = pl.multiple_of(qi * tk + hk, hk)

        def scores(hd):
            hcols = slice(hd * HEAD_SLOT, (hd + 1) * HEAD_SLOT)
            sa = _dot(k_ref[0, pl.ds(off_a, hk), hcols], q_ref[0, hcols, :])
            sb = _dot(k_ref[0, pl.ds(off_b, hk), hcols], q_ref[0, hcols, hk:])
            st_s[hd, 0:hk, 0:hk] = jnp.where(tri, sa[:, 0:hk], NEG_BIG)
            st_s[hd, 0:hk, hk:] = sa[:, hk:]
            st_s[hd, hk:, hk:] = jnp.where(tri, sb, NEG_BIG)

        def softmax_pv(hd):
            m_old = m_s[hd]
            ma = jnp.max(st_s[hd, 0:hk, :], axis=0, keepdims=True)
            mb = jnp.max(st_s[hd, hk:, hk:], axis=0, keepdims=True)
            m_new = jnp.maximum(m_old, jnp.concatenate(
                [ma[:, 0:hk], jnp.maximum(ma[:, hk:], mb)], axis=1))
            alpha = jnp.exp2(m_old - m_new)
            m_s[hd] = m_new
            ea = jnp.exp2((st_s[hd, 0:hk, :] - m_new).astype(BF16))
            eb = jnp.exp2((st_s[hd, hk:, hk:] - m_new[:, hk:]).astype(BF16))
            vrows = slice(hd * VT_ROWS, (hd + 1) * VT_ROWS)
            pva = _dot(vt_ref[0, vrows, pl.ds(off_a, hk)], ea)
            pvb = _dot(vt_ref[0, vrows, pl.ds(off_b, hk)], eb)
            acc_s[vrows, 0:hk] = alpha[:, 0:hk] * acc_s[vrows, 0:hk] + pva[:, 0:hk]
            acc_s[vrows, hk:] = alpha[:, hk:] * acc_s[vrows, hk:] + pva[:, hk:] + pvb

        run_heads(scores, softmax_pv)

    def body(j, carry):
        full_step(j)
        return carry

    lax.fori_loop(0, qi, body, 0)
    diag_step()

    for p in range(MLA_HEADS // 2):
        halves = []
        for hd in (2 * p, 2 * p + 1):
            r0 = hd * VT_ROWS
            halves.append(acc_s[r0:r0 + MLA_V, :] * (1.0 / acc_s[r0 + MLA_V:r0 + MLA_V + 1, :]))
        ot = jnp.concatenate(halves, axis=0)
        o_ref[0, :, p * LANES:(p + 1) * LANES] = ot.T.astype(BF16)


def _attn_call(qt, k, vt):
    b, s, _ = k.shape
    tq = TQ_ATTN
    return pl.pallas_call(
        _attn_kernel,
        grid=(b, s // tq),
        in_specs=[pl.BlockSpec((1, MLA_HEADS * HEAD_SLOT, tq), lambda i, j: (i, 0, j)),
                  pl.BlockSpec((1, s, MLA_HEADS * HEAD_SLOT), lambda i, j: (i, 0, 0)),
                  pl.BlockSpec((1, MLA_HEADS * VT_ROWS, s), lambda i, j: (i, 0, 0))],
        out_specs=pl.BlockSpec((1, tq, D_MLA), lambda i, j: (i, j, 0)),
        out_shape=jax.ShapeDtypeStruct((b, s, D_MLA), BF16),
        scratch_shapes=[pltpu.VMEM((MLA_HEADS, 1, tq), F32),
                        pltpu.VMEM((MLA_HEADS * VT_ROWS, tq), F32),
                        pltpu.VMEM((MLA_HEADS, tq, tq), F32)],
        compiler_params=pltpu.CompilerParams(
            dimension_semantics=("parallel", "arbitrary"), vmem_limit_bytes=VMEM_LIMIT),
        name="mla_attention",
    )(qt, k, vt)


def _gla_kernel(q_ref, k_ref, la_ref, v_ref, r_ref, g_ref, o_ref, st_ref, kpad, bpad, vpad):
    tg = q_ref.shape[1]
    ch = GLA_CHUNK
    sb = GLA_SUB

    @pl.when(pl.program_id(1) == 0)
    def _():
        st_ref[...] = jnp.zeros_like(st_ref)
        kpad[0:sb, :] = jnp.zeros((sb, D_GQK), F32)
        bpad[0:sb, :] = jnp.zeros((sb, D_GQK), F32)
        vpad[0:sb, :] = jnp.zeros((sb, D_GLA), F32)

    q = q_ref[0]
    k = k_ref[0]
    v = v_ref[0]
    ti = lax.broadcasted_iota(jnp.int32, (ch, ch), 0)
    tj = lax.broadcasted_iota(jnp.int32, (ch, ch), 1)
    tri = jnp.where(tj <= ti, 1.0, 0.0).astype(BF16)
    la = la_ref[0] * LOG2_E
    la_hi = la.astype(BF16)
    la_r1 = la - la_hi.astype(F32)
    la_mid = la_r1.astype(BF16)
    la_lo = (la_r1 - la_mid.astype(F32)).astype(BF16)
    b = jnp.concatenate(
        [_dot(tri, la_hi[c * ch:(c + 1) * ch]) + _dot(tri, la_mid[c * ch:(c + 1) * ch])
         + _dot(tri, la_lo[c * ch:(c + 1) * ch]) for c in range(tg // ch)], axis=0)
    kpad[sb:sb + tg, :] = k
    bpad[sb:sb + tg, :] = b
    vpad[sb:sb + tg, :] = v

    er = lax.broadcasted_iota(jnp.int32, (D_GQK, D_GLA), 0)
    ec = lax.broadcasted_iota(jnp.int32, (D_GQK, D_GLA), 1)
    head_sum = jnp.where((er // GLA_DK) == (ec // GLA_DV), 1.0, 0.0).astype(BF16)

    rowmod = lax.broadcasted_iota(jnp.int32, (tg, D_GQK), 0) % sb
    acc = _dot((q * k).astype(BF16), head_sum) * v
    for s in range(1, sb):
        ks = kpad[sb - s:sb - s + tg, :]
        bs = bpad[sb - s:sb - s + tg, :]
        vs = vpad[sb - s:sb - s + tg, :]
        p = jnp.where(rowmod >= s, q * ks * jnp.exp2(b - bs), 0.0)
        acc = acc + _dot(p.astype(BF16), head_sum) * vs

    sr = lax.broadcasted_iota(jnp.int32, (D_GLA, D_GQK), 0)
    sc = lax.broadcasted_iota(jnp.int32, (D_GLA, D_GQK), 1)
    same_head = (sr // GLA_DV) == (sc // GLA_DK)
    gr = lax.broadcasted_iota(jnp.int32, (D_GLA, D_GLA), 0)
    gc = lax.broadcasted_iota(jnp.int32, (D_GLA, D_GLA), 1)
    head_mean = jnp.where((gr // GLA_DV) == (gc // GLA_DV), 1.0 / GLA_DV, 0.0).astype(BF16)
    klane = lax.broadcasted_iota(jnp.int32, (sb, D_GQK), 1) // GLA_DK
    vlane = lax.broadcasted_iota(jnp.int32, (sb, D_GLA), 1) // GLA_DV
    krow = lax.broadcasted_iota(jnp.int32, (ch, D_GQK), 0)
    nch = tg // ch
    chunk = lambda a, c: a[c * ch:(c + 1) * ch]
    vb = v.astype(BF16)

    scores = []
    for c in range(nch):
        bc, qc, kc = chunk(b, c), chunk(q, c), chunk(k, c)
        for blk in range(1, ch // sb):
            lo_r = blk * sb
            r = bc[lo_r:lo_r + 1, :]
            qt = qc[lo_r:lo_r + sb] * jnp.exp2(bc[lo_r:lo_r + sb] - r)
            kt = (kc * jnp.exp2(jnp.where(krow < lo_r, r - bc, NEG_BIG))).astype(BF16)
            qexp = jnp.concatenate([jnp.where(klane == hd, qt, 0.0) for hd in range(GLA_HEADS)], axis=0)
            scores.append(_dot_nt(qexp.astype(BF16), kt))
    upds = []
    for c in range(nch):
        bc = chunk(b, c)
        kd = (chunk(k, c) * jnp.exp2(bc[ch - 1:ch, :] - bc)).astype(BF16)
        upds.append(jnp.where(same_head, _dot_tn(chunk(vb, c), kd), 0.0))
    o_sub = []
    for c in range(nch):
        pieces = [jnp.zeros((sb, D_GLA), F32)]
        for blk in range(1, ch // sb):
            oh = _dot(scores[c * (ch // sb - 1) + blk - 1].astype(BF16), chunk(vb, c))
            piece = jnp.where(vlane == 0, oh[0:sb], 0.0)
            for hd in range(1, GLA_HEADS):
                piece = piece + jnp.where(vlane == hd, oh[hd * sb:(hd + 1) * sb], 0.0)
            pieces.append(piece)
        o_sub.append(jnp.concatenate(pieces, axis=0))
    states = []
    st = st_ref[...]
    for c in range(nch):
        states.append(st.astype(BF16))
        st = st * jnp.exp2(chunk(b, c)[ch - 1:ch, :]) + upds[c]
    st_ref[...] = st
    o = jnp.concatenate(
        [chunk(acc, c) + o_sub[c]
         + _dot_nt((chunk(q, c) * jnp.exp2(chunk(b, c))).astype(BF16), states[c]) for c in range(nch)], axis=0)
    o2 = o * o
    hi = o2.astype(BF16)
    lo = (o2 - hi.astype(F32)).astype(BF16)
    ms = _dot(hi, head_mean) + _dot(lo, head_mean)
    o_ref[0] = (o * lax.rsqrt(ms + EPS) * g_ref[0] * r_ref[0]).astype(BF16)


def _gla_call(gq, gk, gla, gv, gr, g256, l):
    b, s, _ = gq.shape
    tg = TG_GLA
    tspec = lambda n: pl.BlockSpec((1, tg, n), lambda i, j: (i, j, 0))
    return pl.pallas_call(
        _gla_kernel,
        grid=(b, s // tg),
        in_specs=[tspec(D_GQK), tspec(D_GQK), tspec(D_GQK), tspec(D_GLA), tspec(D_GLA),
                  _layer_spec(g256.shape, l)],
        out_specs=tspec(D_GLA),
        out_shape=jax.ShapeDtypeStruct((b, s, D_GLA), BF16),
        scratch_shapes=[pltpu.VMEM((D_GLA, D_GQK), F32),
                        pltpu.VMEM((GLA_SUB + tg, D_GQK), F32),
                        pltpu.VMEM((GLA_SUB + tg, D_GQK), F32),
                        pltpu.VMEM((GLA_SUB + tg, D_GLA), F32)],
        compiler_params=pltpu.CompilerParams(
            dimension_semantics=("parallel", "arbitrary"), vmem_limit_bytes=VMEM_LIMIT),
        name="gla",
    )(gq, gk, gla, gv, gr, g256)


def _post_kernel(x_ref, oa_ref, ob_ref, oc_ref, mod_ref, wo_ref, ng_ref, wup_ref, cw_ref, cb_ref,
                 wdn_ref, fg_ref, o_ref, zbuf, act, *, final):
    tm = x_ref.shape[1]
    halo = 8
    g1 = mod_ref[0, 0, 2:3, :]
    sh2 = mod_ref[0, 0, 3:4, :]
    sc2 = mod_ref[0, 0, 4:5, :]
    g2 = mod_ref[0, 0, 5:6, :]
    mix = (_dot(oa_ref[0], wo_ref[0, 0:D_MLA, :])
           + _dot(ob_ref[0], wo_ref[0, D_MLA:D_MLA + D_SG, :])
           + _dot(oc_ref[0], wo_ref[0, D_MLA + D_SG:, :]))
    x1 = x_ref[0] + g1 * mix
    h2 = _rms(x1, ng_ref[0]) * (1.0 + sc2) + sh2

    @pl.when(pl.program_id(1) == 0)
    def _():
        zbuf[0:halo, :] = jnp.zeros((halo, 2 * D_FF), F32)

    zbuf[halo:halo + tm, :] = _dot(h2.astype(BF16), wup_ref[0])
    cc = FFN_CHUNK
    for j in range(D_FF // cc):
        parts = []
        for cols in (slice(j * cc, (j + 1) * cc), slice(D_FF + j * cc, D_FF + (j + 1) * cc)):
            zc = cb_ref[0, :, cols] + cw_ref[0, 2:3, cols] * zbuf[halo:halo + tm, cols]
            zc = zc + cw_ref[0, 0:1, cols] * zbuf[halo - 2:halo - 2 + tm, cols]
            zc = zc + cw_ref[0, 1:2, cols] * zbuf[halo - 1:halo - 1 + tm, cols]
            parts.append(zc)
        val, gate = parts
        act[:, j * cc:(j + 1) * cc] = (gate * _sigmoid(gate) * val).astype(BF16)
    zbuf[halo - 2:halo, :] = zbuf[halo + tm - 2:halo + tm, :]
    x2 = x1 + g2 * _dot(act[...], wdn_ref[0])
    if final:
        x2 = _rms(x2, fg_ref[...])
    o_ref[0] = x2


def _post_call(x, oa, ob, oc, mod, w, final_g, l, final):
    b, s, d = x.shape
    tm = TM_POST
    tspec = lambda n: pl.BlockSpec((1, tm, n), lambda i, j: (i, j, 0))
    consts = [w["wo"], w["ng2"], w["wup"], w["cw"], w["cb"], w["wdn"]]
    return pl.pallas_call(
        functools.partial(_post_kernel, final=final),
        grid=(b, s // tm),
        in_specs=[tspec(d), tspec(D_MLA), tspec(D_SG), tspec(D_GLA), _mod_spec(d, l)]
                 + [_layer_spec(a.shape, l) for a in consts] + [_const_spec(final_g.shape)],
        out_specs=tspec(d),
        out_shape=jax.ShapeDtypeStruct((b, s, d), F32),
        scratch_shapes=[pltpu.VMEM((8 + tm, 2 * D_FF), F32), pltpu.VMEM((tm, D_FF), BF16)],
        compiler_params=pltpu.CompilerParams(
            dimension_semantics=("parallel", "arbitrary"), vmem_limit_bytes=VMEM_LIMIT),
        name="post_ffn",
    )(x, oa, ob, oc, mod, *consts, final_g)


def _rot_half_cols(w):
    half = MLA_ROPE // 2
    return jnp.concatenate([-w[..., half:], w[..., :half]], axis=-1)


def _stacked_weights(p):
    depth, d, _ = p["w_in"].shape
    splits = (MLA_Q_LORA, MLA_KV_LORA, MLA_ROPE, D_SG, D_SG, D_GQK, D_GQK, D_GLA, GLA_GATE_RANK, D_GLA)
    offs = [0]
    for n in splits:
        offs.append(offs[-1] + n)
    w_in = p["w_in"]
    col = lambda i: w_in[..., offs[i]:offs[i + 1]]
    win = jnp.concatenate([
        w_in[..., :offs[2]],
        col(8), col(2), _rot_half_cols(col(2)), jnp.zeros((depth, d, LANES - MISC_KRR - MLA_ROPE), F32),
        w_in[..., offs[3]:offs[8]], col(9)], axis=-1).astype(BF16)
    assert win.shape[-1] == Z_COLS

    keep3 = ((0, 0), (0, 0), (0, 0))
    wq = p["mla_w_uq"].astype(BF16).reshape(depth, MLA_Q_LORA, MLA_HEADS, MLA_NOPE + MLA_ROPE)
    nope, rope = wq[..., :MLA_NOPE], wq[..., MLA_NOPE:]
    pad_cols = jnp.zeros(rope.shape[:-1] + (HEAD_SLOT - MLA_NOPE - MLA_ROPE,), BF16)
    odd_head = (jnp.arange(MLA_HEADS) % 2 == 1)[None, None, :, None]
    wuq = jnp.where(odd_head, jnp.concatenate([rope, pad_cols, nope], axis=-1),
                    jnp.concatenate([nope, rope, pad_cols], axis=-1))
    wuq = jnp.swapaxes(wuq.reshape(depth, MLA_Q_LORA, MLA_HEADS * HEAD_SLOT), 1, 2)
    wuqr = _rot_half_cols(rope).reshape(depth, MLA_Q_LORA, MLA_HEADS * MLA_ROPE)
    wuqr = jnp.swapaxes(wuqr, 1, 2)
    wkv = p["mla_w_ukv"].astype(BF16).reshape(depth, MLA_KV_LORA, MLA_HEADS, MLA_NOPE + MLA_V)
    wuk = wkv[..., :MLA_NOPE].reshape(depth, MLA_KV_LORA, MLA_HEADS * MLA_NOPE)
    wuv = jnp.pad(wkv[..., MLA_NOPE:], keep3 + ((0, VT_ROWS - MLA_V),))
    wuv = jnp.swapaxes(wuv.reshape(depth, MLA_KV_LORA, MLA_HEADS * VT_ROWS), 1, 2)

    bs = jnp.repeat(jnp.swapaxes(p["sg_b_s"], 1, 2), SG_HEAD_DIM, axis=2)
    wg2 = jnp.pad(p["gla_w_gate2"].astype(BF16), ((0, 0), (0, LANES - GLA_GATE_RANK), (0, 0)))
    row = lambda a: a.reshape(depth, 1, -1)
    return {
        "ng1": row(p["norm_mix_g"]), "win": win,
        "qg": row(p["mla_q_norm_g"]), "kvg": row(p["mla_kv_norm_g"]),
        "wuq": wuq, "wuqr": wuqr, "wuk": wuk, "wuv": wuv,
        "lng": row(p["sg_ln_g"]), "lnb": row(p["sg_ln_b"]), "ws": p["sg_w_s"], "bs": bs,
        "wg2": wg2, "bg": row(p["gla_b_gate"]),
        "gng": row(jnp.tile(p["gla_norm_g"], (1, GLA_HEADS))),
        "wo": p["w_out"].astype(BF16), "ng2": row(p["norm_ffn_g"]),
        "wup": p["ffn_w_up"].astype(BF16), "cw": p["ffn_conv_w"],
        "cb": row(p["ffn_conv_b"]), "wdn": p["ffn_w_down"].astype(BF16),
    }


def kernel(x, c, positions, mod_w, mod_b, norm_mix_g, norm_ffn_g, w_in, mla_q_norm_g, mla_kv_norm_g, mla_w_uq, mla_w_ukv, sg_ln_g, sg_ln_b, sg_w_s, sg_b_s, gla_w_gate2, gla_b_gate, gla_norm_g, w_out, ffn_w_up, ffn_conv_w, ffn_conv_b, ffn_w_down, final_norm_g):
    p = dict(norm_mix_g=norm_mix_g, norm_ffn_g=norm_ffn_g, w_in=w_in, mla_q_norm_g=mla_q_norm_g,
             mla_kv_norm_g=mla_kv_norm_g, mla_w_uq=mla_w_uq, mla_w_ukv=mla_w_ukv, sg_ln_g=sg_ln_g,
             sg_ln_b=sg_ln_b, sg_w_s=sg_w_s, sg_b_s=sg_b_s, gla_w_gate2=gla_w_gate2,
             gla_b_gate=gla_b_gate, gla_norm_g=gla_norm_g, w_out=w_out, ffn_w_up=ffn_w_up,
             ffn_conv_w=ffn_conv_w, ffn_conv_b=ffn_conv_b, ffn_w_down=ffn_w_down)
    depth = mod_w.shape[0]
    b = x.shape[0]
    mod = _modulation(c, mod_w, mod_b).reshape(depth, b, 6, D_MODEL)
    tables = _rope_tables(positions)
    fg = final_norm_g.reshape(1, -1)
    w = _stacked_weights(p)
    for l in range(depth):
        q, k, v, o_sg, gq, gk, gla, gv, gr = _pre_call(x, mod, w, tables, l)
        o_mla = _attn_call(q, k, v)
        o_gla = _gla_call(gq, gk, gla, gv, gr, w["gng"], l)
        x = _post_call(x, o_mla, o_sg, o_gla, mod, w, fg, l, final=(l == depth - 1))
    return x
```

```python
import functools

import jax
import jax.numpy as jnp
from jax import lax
from jax.experimental import pallas as pl
from jax.experimental.pallas import tpu as pltpu

D_MODEL = 1024
MLA_HEADS = 8
MLA_NOPE = 64
MLA_ROPE = 32
MLA_V = 64
MLA_Q_LORA = 256
MLA_KV_LORA = 128
ROPE_THETA = 10000.0
SG_HEADS = 4
SG_HEAD_DIM = 64
SG_CHUNK = 128
GLA_HEADS = 4
GLA_DK = 32
GLA_DV = 64
GLA_GATE_RANK = 16
GLA_GATE_TAU = 16.0
GLA_CHUNK = 64
GLA_SUB = 8
D_FF = 2816
CONV_WIDTH = 3
EPS = 1e-6

D_MLA = MLA_HEADS * MLA_V
D_SG = SG_HEADS * SG_HEAD_DIM
D_GLA = GLA_HEADS * GLA_DV
D_GQK = GLA_HEADS * GLA_DK

LANES = 128
HEAD_SLOT = LANES
ROPE_LO = MLA_NOPE
VT_ROWS = MLA_V + 16
VMEM_LIMIT = 56 * 1024 * 1024

Z_CQ = 0
Z_CKV = Z_CQ + MLA_Q_LORA
Z_MISC = Z_CKV + MLA_KV_LORA
Z_SU = Z_MISC + LANES
Z_SV = Z_SU + D_SG
Z_GQ = Z_SV + D_SG
Z_GK = Z_GQ + D_GQK
Z_GV = Z_GK + D_GQK
Z_GR = Z_GV + D_GLA
Z_COLS = Z_GR + D_GLA
MISC_KR = GLA_GATE_RANK
MISC_KRR = MISC_KR + MLA_ROPE
ODD_NOPE_LO = HEAD_SLOT - MLA_NOPE

TM_PRE = 1024
TQ_ATTN = 512
TG_GLA = 1024
ATTN_LOOKAHEAD = 2
TM_POST = 512
FFN_CHUNK = 256

F32 = jnp.float32
BF16 = jnp.bfloat16
NEG_BIG = -1e30
LOG2_E = 1.4426950408889634


def _dot(a, b):
    return jnp.dot(a, b, preferred_element_type=F32)


def _dot_nt(a, b):
    return lax.dot_general(a, b, (((1,), (1,)), ((), ())), preferred_element_type=F32)


def _dot_tn(a, b):
    return lax.dot_general(a, b, (((0,), (0,)), ((), ())), preferred_element_type=F32)


def _sigmoid(x):
    return 1.0 / (1.0 + jnp.exp(-x))


def _gelu_tanh(x):
    return 0.5 * x * (1.0 + jnp.tanh(0.7978845608028654 * (x + 0.044715 * (x * x * x))))


def _log_sigmoid(x):
    return jnp.minimum(x, 0.0) - jnp.log1p(jnp.exp(-jnp.abs(x)))


def _rms(x, g):
    return x * lax.rsqrt(jnp.mean(x * x, axis=-1, keepdims=True) + EPS) * g


def _const_spec(shape):
    zeros = (0,) * len(shape)
    return pl.BlockSpec(shape, lambda *_: zeros, pipeline_mode=pl.Buffered(1))


def _layer_spec(shape, l):
    idx = (l,) + (0,) * (len(shape) - 1)
    return pl.BlockSpec((1,) + tuple(shape[1:]), lambda *_: idx, pipeline_mode=pl.Buffered(1))


def _mod_spec(d, l):
    return pl.BlockSpec((1, 1, 6, d), lambda i, j: (l, i, 0, 0))


def _mod_kernel(c_ref, w_ref, b_ref, o_ref):
    c = c_ref[...]
    ca = c * _sigmoid(c)
    nb = ca.shape[0]
    a_hi = ca.astype(BF16)
    a_lo = (ca - a_hi.astype(F32)).astype(BF16)
    w = w_ref[0]
    w_hi = w.astype(BF16)
    w_lo = (w - w_hi.astype(F32)).astype(BF16)
    both = _dot(jnp.concatenate([a_hi, a_lo], axis=0), w_hi)
    o_ref[0] = both[0:nb] + both[nb:2 * nb] + _dot(a_hi, w_lo) + b_ref[0]


def _modulation(c, mod_w, mod_b):
    depth, d, n = mod_w.shape
    b = c.shape[0]
    tn = 2048
    return pl.pallas_call(
        _mod_kernel,
        grid=(depth, n // tn),
        in_specs=[
            pl.BlockSpec((b, d), lambda l, j: (0, 0)),
            pl.BlockSpec((1, d, tn), lambda l, j: (l, 0, j)),
            pl.BlockSpec((1, 1, tn), lambda l, j: (l, 0, j)),
        ],
        out_specs=pl.BlockSpec((1, b, tn), lambda l, j: (l, 0, j)),
        out_shape=jax.ShapeDtypeStruct((depth, b, n), F32),
        compiler_params=pltpu.CompilerParams(vmem_limit_bytes=VMEM_LIMIT),
        name="modulation",
    )(c, mod_w, mod_b.reshape(depth, 1, n))


def _rope_kernel(post_ref, invft_ref, cos_ref, sin_ref, cost_ref, sint_ref):
    angt = invft_ref[...] * post_ref[0]
    cost = jnp.cos(angt)
    sint = jnp.sin(angt)
    cost_ref[0] = cost
    sint_ref[0] = sint
    tm = angt.shape[1]
    zrows = lambda n: jnp.zeros((n, tm), F32)
    cos_ref[0] = jnp.concatenate([zrows(MISC_KR), cost, zrows(LANES - MISC_KR - MLA_ROPE)], axis=0).T
    sin_ref[0] = jnp.concatenate([zrows(MISC_KRR), sint, zrows(LANES - MISC_KRR - MLA_ROPE)], axis=0).T


def _rope_tables(positions):
    b, s = positions.shape
    tm = 512
    inv_freq = ROPE_THETA ** (-jnp.arange(0, MLA_ROPE, 2, dtype=F32) / MLA_ROPE)
    inv2 = jnp.concatenate([inv_freq, inv_freq])
    posf = positions.astype(F32)
    out = jax.ShapeDtypeStruct((b, s, LANES), F32)
    outt = jax.ShapeDtypeStruct((b, MLA_ROPE, s), F32)
    return pl.pallas_call(
        _rope_kernel,
        grid=(b, s // tm),
        in_specs=[pl.BlockSpec((1, 1, tm), lambda i, j: (i, 0, j)),
                  pl.BlockSpec((MLA_ROPE, 1), lambda i, j: (0, 0))],
        out_specs=[pl.BlockSpec((1, tm, LANES), lambda i, j: (i, j, 0))] * 2
                  + [pl.BlockSpec((1, MLA_ROPE, tm), lambda i, j: (i, 0, j))] * 2,
        out_shape=[out, out, outt, outt],
        name="rope_tables",
    )(posf.reshape(b, 1, s), inv2.reshape(MLA_ROPE, 1))


def _pre_kernel(x_ref, mod_ref, ng_ref, win_ref, qg_ref, kvg_ref, wuq_ref, wuqr_ref, wuk_ref,
                wuv_ref, cos_ref, sin_ref, cost_ref, sint_ref, lng_ref, lnb_ref, ws_ref, bs_ref,
                wg2_ref, bg_ref,
                q_out, k_out, v_out, sg_out, gq_out, gk_out, gla_out, gv_out, gr_out):
    tm = x_ref.shape[1]
    x = x_ref[0]
    sh1 = mod_ref[0, 0, 0:1, :]
    sc1 = mod_ref[0, 0, 1:2, :]
    h = _rms(x, ng_ref[0]) * (1.0 + sc1) + sh1
    z = _dot(h.astype(BF16), win_ref[0])

    cq = _rms(z[:, Z_CQ:Z_CQ + MLA_Q_LORA], qg_ref[0]).astype(BF16)
    ckv = _rms(z[:, Z_CKV:Z_CKV + MLA_KV_LORA], kvg_ref[0]).astype(BF16)
    qat = _dot_nt(wuq_ref[0], cq)
    qrt = _dot_nt(wuqr_ref[0], cq)
    cost = cost_ref[0]
    sint = sint_ref[0]
    scale = float(MLA_NOPE + MLA_ROPE) ** -0.5 * LOG2_E
    zero_rows = jnp.zeros((HEAD_SLOT - MLA_NOPE - MLA_ROPE, tm), BF16)
    for hd in range(MLA_HEADS):
        r0 = hd * HEAD_SLOT
        nope0, rope0 = (r0, r0 + ROPE_LO) if hd % 2 == 0 else (r0 + ODD_NOPE_LO, r0)
        rot = qrt[hd * MLA_ROPE:(hd + 1) * MLA_ROPE]
        q_out[0, nope0:nope0 + MLA_NOPE, :] = (qat[nope0:nope0 + MLA_NOPE] * scale).astype(BF16)
        q_out[0, rope0:rope0 + MLA_ROPE, :] = (
            (qat[rope0:rope0 + MLA_ROPE] * cost + rot * sint) * scale).astype(BF16)
        q_out[0, rope0 + MLA_ROPE:rope0 + MLA_ROPE + zero_rows.shape[0], :] = zero_rows
    misc = z[:, Z_MISC:Z_MISC + LANES]
    kc = misc * cos_ref[0]
    ks = misc * sin_ref[0]
    kr_even = pltpu.roll(kc, ROPE_LO - MISC_KR, 1) + pltpu.roll(ks, ROPE_LO - MISC_KRR, 1)
    kr_odd = pltpu.roll(kc, LANES - MISC_KR, 1) + pltpu.roll(ks, LANES - MISC_KRR, 1)
    kn = _dot(ckv, wuk_ref[0])
    klane = lax.broadcasted_iota(jnp.int32, (tm, LANES), 1)
    for hd in range(MLA_HEADS):
        pair = kn[:, (hd // 2) * LANES:(hd // 2 + 1) * LANES]
        slot = (jnp.where(klane < MLA_NOPE, pair, kr_even) if hd % 2 == 0
                else jnp.where(klane >= ODD_NOPE_LO, pair, kr_odd))
        k_out[0, :, hd * HEAD_SLOT:(hd + 1) * HEAD_SLOT] = slot.astype(BF16)
    vt = _dot_nt(wuv_ref[0], ckv)
    vrow = lax.broadcasted_iota(jnp.int32, vt.shape, 0)
    v_out[0] = jnp.where(vrow % VT_ROWS == MLA_V, 1.0, vt).astype(BF16)

    u = _gelu_tanh(z[:, Z_SU:Z_SU + D_SG])
    gv = _gelu_tanh(z[:, Z_SV:Z_SV + D_SG])
    mu = jnp.mean(gv, axis=-1, keepdims=True)
    dv = gv - mu
    var = jnp.mean(dv * dv, axis=-1, keepdims=True)
    vn = (dv * lax.rsqrt(var + EPS) * lng_ref[0] + lnb_ref[0]).astype(BF16)
    trow = lax.broadcasted_iota(jnp.int32, (SG_CHUNK, SG_CHUNK), 0)
    tcol = lax.broadcasted_iota(jnp.int32, (SG_CHUNK, SG_CHUNK), 1)
    wmix = [jnp.where(trow >= tcol, ws_ref[0, i], 0.0).astype(BF16) for i in range(SG_HEADS)]
    lane = lax.broadcasted_iota(jnp.int32, (SG_CHUNK, LANES), 1)
    lo_half = lane < SG_HEAD_DIM
    nchunk = tm // SG_CHUNK
    for p in range(D_SG // LANES):
        cols = slice(p * LANES, (p + 1) * LANES)
        vp = jnp.concatenate([vn[c * SG_CHUNK:(c + 1) * SG_CHUNK, cols] for c in range(nchunk)], axis=1)
        m_lo = _dot(wmix[2 * p], vp)
        m_hi = _dot(wmix[2 * p + 1], vp)
        for c in range(nchunk):
            rows = slice(c * SG_CHUNK, (c + 1) * SG_CHUNK)
            blk = slice(c * LANES, (c + 1) * LANES)
            mixed = jnp.where(lo_half, m_lo[:, blk], m_hi[:, blk])
            sg_out[0, rows, cols] = (u[rows, cols] * (mixed + bs_ref[0, :, cols])).astype(BF16)

    gq_out[0] = z[:, Z_GQ:Z_GQ + D_GQK] * (float(GLA_DK) ** -0.5)
    gk_out[0] = z[:, Z_GK:Z_GK + D_GQK]
    gv_out[0] = z[:, Z_GV:Z_GV + D_GLA]
    gate = _dot(misc.astype(BF16), wg2_ref[0]) + bg_ref[0]
    gla_out[0] = _log_sigmoid(gate) * (1.0 / GLA_GATE_TAU)
    gr = z[:, Z_GR:Z_GR + D_GLA]
    gr_out[0] = gr * _sigmoid(gr)


def _pre_call(x, mod, w, tables, l):
    b, s, d = x.shape
    tm = TM_PRE
    tok = lambda n, dt: jax.ShapeDtypeStruct((b, s, n), dt)
    tspec = lambda n: pl.BlockSpec((1, tm, n), lambda i, j: (i, j, 0))
    tspec_t = lambda n: pl.BlockSpec((1, n, tm), lambda i, j: (i, 0, j))
    consts = [w["ng1"], w["win"], w["qg"], w["kvg"], w["wuq"], w["wuqr"], w["wuk"], w["wuv"]]
    consts2 = [w["lng"], w["lnb"], w["ws"], w["bs"], w["wg2"], w["bg"]]
    in_specs = ([tspec(d), _mod_spec(d, l)]
                + [_layer_spec(a.shape, l) for a in consts]
                + [tspec(LANES), tspec(LANES), tspec_t(MLA_ROPE), tspec_t(MLA_ROPE)]
                + [_layer_spec(a.shape, l) for a in consts2])
    outs = [(MLA_HEADS * HEAD_SLOT, BF16), (MLA_HEADS * HEAD_SLOT, BF16), (D_MLA, BF16),
            (D_SG, BF16), (D_GQK, F32), (D_GQK, F32), (D_GQK, F32), (D_GLA, F32), (D_GLA, F32)]
    out_specs = [tspec(n) for n, _ in outs]
    out_shape = [tok(n, dt) for n, dt in outs]
    out_specs[0] = tspec_t(MLA_HEADS * HEAD_SLOT)
    out_shape[0] = jax.ShapeDtypeStruct((b, MLA_HEADS * HEAD_SLOT, s), BF16)
    out_specs[2] = tspec_t(MLA_HEADS * VT_ROWS)
    out_shape[2] = jax.ShapeDtypeStruct((b, MLA_HEADS * VT_ROWS, s), BF16)
    return pl.pallas_call(
        _pre_kernel,
        grid=(b, s // tm),
        in_specs=in_specs,
        out_specs=out_specs,
        out_shape=out_shape,
        compiler_params=pltpu.CompilerParams(
            dimension_semantics=("parallel", "parallel"), vmem_limit_bytes=VMEM_LIMIT),
        name="pre_mix",
    )(x, mod, *consts, *tables, *consts2)


def _attn_kernel(q_ref, k_ref, vt_ref, o_ref, m_s, acc_s, st_s):
    tq = q_ref.shape[2]
    tk = tq
    qi = pl.program_id(1)
    hk = tk // 2
    tri = (lax.broadcasted_iota(jnp.int32, (hk, hk), 0)
           <= lax.broadcasted_iota(jnp.int32, (hk, hk), 1))

    m_s[...] = jnp.full(m_s.shape, NEG_BIG, F32)
    acc_s[...] = jnp.zeros(acc_s.shape, F32)

    def run_heads(scores, softmax_pv):
        for hd in range(ATTN_LOOKAHEAD):
            scores(hd)
        for hd in range(MLA_HEADS):
            if hd + ATTN_LOOKAHEAD < MLA_HEADS:
                scores(hd + ATTN_LOOKAHEAD)
            softmax_pv(hd)

    def full_step(j):
        off = pl.multiple_of(j * tk, tk)

        def scores(hd):
            hcols = slice(hd * HEAD_SLOT, (hd + 1) * HEAD_SLOT)
            st_s[hd] = _dot(k_ref[0, pl.ds(off, tk), hcols], q_ref[0, hcols, :])

        def softmax_pv(hd):
            m_old = m_s[hd]
            m_new = jnp.maximum(m_old, jnp.max(st_s[hd], axis=0, keepdims=True))
            alpha = jnp.exp2(m_old - m_new)
            m_s[hd] = m_new
            e = jnp.exp2((st_s[hd] - m_new).astype(BF16))
            vrows = slice(hd * VT_ROWS, (hd + 1) * VT_ROWS)
            pv = _dot(vt_ref[0, vrows, pl.ds(off, tk)], e)
            acc_s[vrows, :] = alpha * acc_s[vrows, :] + pv

        run_heads(scores, softmax_pv)

    def diag_step():
        off_a = pl.multiple_of(qi * tk, tk)
        off_b = pl.multiple_of(qi * tk + hk, hk)

        def scores(hd):
            hcols = slice(hd * HEAD_SLOT, (hd + 1) * HEAD_SLOT)
            sa = _dot(k_ref[0, pl.ds(off_a, hk), hcols], q_ref[0, hcols, :])
            sb = _dot(k_ref[0, pl.ds(off_b, hk), hcols], q_ref[0, hcols, hk:])
            st_s[hd, 0:hk, 0:hk] = jnp.where(tri, sa[:, 0:hk], NEG_BIG)
            st_s[hd, 0:hk, hk:] = sa[:, hk:]
            st_s[hd, hk:, hk:] = jnp.where(tri, sb, NEG_BIG)

        def softmax_pv(hd):
            m_old = m_s[hd]
            ma = jnp.max(st_s[hd, 0:hk, :], axis=0, keepdims=True)
            mb = jnp.max(st_s[hd, hk:, hk:], axis=0, keepdims=True)
            m_new = jnp.maximum(m_old, jnp.concatenate(
                [ma[:, 0:hk], jnp.maximum(ma[:, hk:], mb)], axis=1))
            alpha = jnp.exp2(m_old - m_new)
            m_s[hd] = m_new
            ea = jnp.exp2((st_s[hd, 0:hk, :] - m_new).astype(BF16))
            eb = jnp.exp2((st_s[hd, hk:, hk:] - m_new[:, hk:]).astype(BF16))
            vrows = slice(hd * VT_ROWS, (hd + 1) * VT_ROWS)
            pva = _dot(vt_ref[0, vrows, pl.ds(off_a, hk)], ea)
            pvb = _dot(vt_ref[0, vrows, pl.ds(off_b, hk)], eb)
            acc_s[vrows, 0:hk] = alpha[:, 0:hk] * acc_s[vrows, 0:hk] + pva[:, 0:hk]
            acc_s[vrows, hk:] = alpha[:, hk:] * acc_s[vrows, hk:] + pva[:, hk:] + pvb

        run_heads(scores, softmax_pv)

    def body(j, carry):
        full_step(j)
        return carry

    lax.fori_loop(0, qi, body, 0)
    diag_step()

    for p in range(MLA_HEADS // 2):
        halves = []
        for hd in (2 * p, 2 * p + 1):
            r0 = hd * VT_ROWS
            halves.append(acc_s[r0:r0 + MLA_V, :] * (1.0 / acc_s[r0 + MLA_V:r0 + MLA_V + 1, :]))
        ot = jnp.concatenate(halves, axis=0)
        o_ref[0, :, p * LANES:(p + 1) * LANES] = ot.T.astype(BF16)


def _attn_call(qt, k, vt):
    b, s, _ = k.shape
    tq = TQ_ATTN
    return pl.pallas_call(
        _attn_kernel,
        grid=(b, s // tq),
        in_specs=[pl.BlockSpec((1, MLA_HEADS * HEAD_SLOT, tq), lambda i, j: (i, 0, j)),
                  pl.BlockSpec((1, s, MLA_HEADS * HEAD_SLOT), lambda i, j: (i, 0, 0)),
                  pl.BlockSpec((1, MLA_HEADS * VT_ROWS, s), lambda i, j: (i, 0, 0))],
        out_specs=pl.BlockSpec((1, tq, D_MLA), lambda i, j: (i, j, 0)),
        out_shape=jax.ShapeDtypeStruct((b, s, D_MLA), BF16),
        scratch_shapes=[pltpu.VMEM((MLA_HEADS, 1, tq), F32),
                        pltpu.VMEM((MLA_HEADS * VT_ROWS, tq), F32),
                        pltpu.VMEM((MLA_HEADS, tq, tq), F32)],
        compiler_params=pltpu.CompilerParams(
            dimension_semantics=("parallel", "arbitrary"), vmem_limit_bytes=VMEM_LIMIT),
        name="mla_attention",
    )(qt, k, vt)


def _gla_kernel(q_ref, k_ref, la_ref, v_ref, r_ref, g_ref, o_ref, st_ref, kpad, bpad, vpad):
    tg = q_ref.shape[1]
    ch = GLA_CHUNK
    sb = GLA_SUB

    @pl.when(pl.program_id(1) == 0)
    def _():
        st_ref[...] = jnp.zeros_like(st_ref)
        kpad[0:sb, :] = jnp.zeros((sb, D_GQK), F32)
        bpad[0:sb, :] = jnp.zeros((sb, D_GQK), F32)
        vpad[0:sb, :] = jnp.zeros((sb, D_GLA), F32)

    q = q_ref[0]
    k = k_ref[0]
    v = v_ref[0]
    ti = lax.broadcasted_iota(jnp.int32, (ch, ch), 0)
    tj = lax.broadcasted_iota(jnp.int32, (ch, ch), 1)
    tri = jnp.where(tj <= ti, 1.0, 0.0).astype(BF16)
    la = la_ref[0] * LOG2_E
    la_hi = la.astype(BF16)
    la_r1 = la - la_hi.astype(F32)
    la_mid = la_r1.astype(BF16)
    la_lo = (la_r1 - la_mid.astype(F32)).astype(BF16)
    b = jnp.concatenate(
        [_dot(tri, la_hi[c * ch:(c + 1) * ch]) + _dot(tri, la_mid[c * ch:(c + 1) * ch])
         + _dot(tri, la_lo[c * ch:(c + 1) * ch]) for c in range(tg // ch)], axis=0)
    kpad[sb:sb + tg, :] = k
    bpad[sb:sb + tg, :] = b
    vpad[sb:sb + tg, :] = v

    er = lax.broadcasted_iota(jnp.int32, (D_GQK, D_GLA), 0)
    ec = lax.broadcasted_iota(jnp.int32, (D_GQK, D_GLA), 1)
    head_sum = jnp.where((er // GLA_DK) == (ec // GLA_DV), 1.0, 0.0).astype(BF16)

    rowmod = lax.broadcasted_iota(jnp.int32, (tg, D_GQK), 0) % sb
    acc = _dot((q * k).astype(BF16), head_sum) * v
    for s in range(1, sb):
        ks = kpad[sb - s:sb - s + tg, :]
        bs = bpad[sb - s:sb - s + tg, :]
        vs = vpad[sb - s:sb - s + tg, :]
        p = jnp.where(rowmod >= s, q * ks * jnp.exp2(b - bs), 0.0)
        acc = acc + _dot(p.astype(BF16), head_sum) * vs

    sr = lax.broadcasted_iota(jnp.int32, (D_GLA, D_GQK), 0)
    sc = lax.broadcasted_iota(jnp.int32, (D_GLA, D_GQK), 1)
    same_head = (sr // GLA_DV) == (sc // GLA_DK)
    gr = lax.broadcasted_iota(jnp.int32, (D_GLA, D_GLA), 0)
    gc = lax.broadcasted_iota(jnp.int32, (D_GLA, D_GLA), 1)
    head_mean = jnp.where((gr // GLA_DV) == (gc // GLA_DV), 1.0 / GLA_DV, 0.0).astype(BF16)
    klane = lax.broadcasted_iota(jnp.int32, (sb, D_GQK), 1) // GLA_DK
    vlane = lax.broadcasted_iota(jnp.int32, (sb, D_GLA), 1) // GLA_DV
    krow = lax.broadcasted_iota(jnp.int32, (ch, D_GQK), 0)
    nch = tg // ch
    chunk = lambda a, c: a[c * ch:(c + 1) * ch]
    vb = v.astype(BF16)

    scores = []
    for c in range(nch):
        bc, qc, kc = chunk(b, c), chunk(q, c), chunk(k, c)
        for blk in range(1, ch // sb):
            lo_r = blk * sb
            r = bc[lo_r:lo_r + 1, :]
            qt = qc[lo_r:lo_r + sb] * jnp.exp2(bc[lo_r:lo_r + sb] - r)
            kt = (kc * jnp.exp2(jnp.where(krow < lo_r, r - bc, NEG_BIG))).astype(BF16)
            qexp = jnp.concatenate([jnp.where(klane == hd, qt, 0.0) for hd in range(GLA_HEADS)], axis=0)
            scores.append(_dot_nt(qexp.astype(BF16), kt))
    upds = []
    for c in range(nch):
        bc = chunk(b, c)
        kd = (chunk(k, c) * jnp.exp2(bc[ch - 1:ch, :] - bc)).astype(BF16)
        upds.append(jnp.where(same_head, _dot_tn(chunk(vb, c), kd), 0.0))
    o_sub = []
    for c in range(nch):
        pieces = [jnp.zeros((sb, D_GLA), F32)]
        for blk in range(1, ch // sb):
            oh = _dot(scores[c * (ch // sb - 1) + blk - 1].astype(BF16), chunk(vb, c))
            piece = jnp.where(vlane == 0, oh[0:sb], 0.0)
            for hd in range(1, GLA_HEADS):
                piece = piece + jnp.where(vlane == hd, oh[hd * sb:(hd + 1) * sb], 0.0)
            pieces.append(piece)
        o_sub.append(jnp.concatenate(pieces, axis=0))
    states = []
    st = st_ref[...]
    for c in range(nch):
        states.append(st.astype(BF16))
        st = st * jnp.exp2(chunk(b, c)[ch - 1:ch, :]) + upds[c]
    st_ref[...] = st
    o = jnp.concatenate(
        [chunk(acc, c) + o_sub[c]
         + _dot_nt((chunk(q, c) * jnp.exp2(chunk(b, c))).astype(BF16), states[c]) for c in range(nch)], axis=0)
    o2 = o * o
    hi = o2.astype(BF16)
    lo = (o2 - hi.astype(F32)).astype(BF16)
    ms = _dot(hi, head_mean) + _dot(lo, head_mean)
    o_ref[0] = (o * lax.rsqrt(ms + EPS) * g_ref[0] * r_ref[0]).astype(BF16)


def _gla_call(gq, gk, gla, gv, gr, g256, l):
    b, s, _ = gq.shape
    tg = TG_GLA
    tspec = lambda n: pl.BlockSpec((1, tg, n), lambda i, j: (i, j, 0))
    return pl.pallas_call(
        _gla_kernel,
        grid=(b, s // tg),
        in_specs=[tspec(D_GQK), tspec(D_GQK), tspec(D_GQK), tspec(D_GLA), tspec(D_GLA),
                  _layer_spec(g256.shape, l)],
        out_specs=tspec(D_GLA),
        out_shape=jax.ShapeDtypeStruct((b, s, D_GLA), BF16),
        scratch_shapes=[pltpu.VMEM((D_GLA, D_GQK), F32),
                        pltpu.VMEM((GLA_SUB + tg, D_GQK), F32),
                        pltpu.VMEM((GLA_SUB + tg, D_GQK), F32),
                        pltpu.VMEM((GLA_SUB + tg, D_GLA), F32)],
        compiler_params=pltpu.CompilerParams(
            dimension_semantics=("parallel", "arbitrary"), vmem_limit_bytes=VMEM_LIMIT),
        name="gla",
    )(gq, gk, gla, gv, gr, g256)


def _post_kernel(x_ref, oa_ref, ob_ref, oc_ref, mod_ref, wo_ref, ng_ref, wup_ref, cw_ref, cb_ref,
                 wdn_ref, fg_ref, o_ref, zbuf, act, *, final):
    tm = x_ref.shape[1]
    halo = 8
    g1 = mod_ref[0, 0, 2:3, :]
    sh2 = mod_ref[0, 0, 3:4, :]
    sc2 = mod_ref[0, 0, 4:5, :]
    g2 = mod_ref[0, 0, 5:6, :]
    mix = (_dot(oa_ref[0], wo_ref[0, 0:D_MLA, :])
           + _dot(ob_ref[0], wo_ref[0, D_MLA:D_MLA + D_SG, :])
           + _dot(oc_ref[0], wo_ref[0, D_MLA + D_SG:, :]))
    x1 = x_ref[0] + g1 * mix
    h2 = _rms(x1, ng_ref[0]) * (1.0 + sc2) + sh2

    @pl.when(pl.program_id(1) == 0)
    def _():
        zbuf[0:halo, :] = jnp.zeros((halo, 2 * D_FF), F32)

    zbuf[halo:halo + tm, :] = _dot(h2.astype(BF16), wup_ref[0])
    cc = FFN_CHUNK
    for j in range(D_FF // cc):
        parts = []
        for cols in (slice(j * cc, (j + 1) * cc), slice(D_FF + j * cc, D_FF + (j + 1) * cc)):
            zc = cb_ref[0, :, cols] + cw_ref[0, 2:3, cols] * zbuf[halo:halo + tm, cols]
            zc = zc + cw_ref[0, 0:1, cols] * zbuf[halo - 2:halo - 2 + tm, cols]
            zc = zc + cw_ref[0, 1:2, cols] * zbuf[halo - 1:halo - 1 + tm, cols]
            parts.append(zc)
        val, gate = parts
        act[:, j * cc:(j + 1) * cc] = (gate * _sigmoid(gate) * val).astype(BF16)
    zbuf[halo - 2:halo, :] = zbuf[halo + tm - 2:halo + tm, :]
    x2 = x1 + g2 * _dot(act[...], wdn_ref[0])
    if final:
        x2 = _rms(x2, fg_ref[...])
    o_ref[0] = x2


def _post_call(x, oa, ob, oc, mod, w, final_g, l, final):
    b, s, d = x.shape
    tm = TM_POST
    tspec = lambda n: pl.BlockSpec((1, tm, n), lambda i, j: (i, j, 0))
    consts = [w["wo"], w["ng2"], w["wup"], w["cw"], w["cb"], w["wdn"]]
    return pl.pallas_call(
        functools.partial(_post_kernel, final=final),
        grid=(b, s // tm),
        in_specs=[tspec(d), tspec(D_MLA), tspec(D_SG), tspec(D_GLA), _mod_spec(d, l)]
                 + [_layer_spec(a.shape, l) for a in consts] + [_const_spec(final_g.shape)],
        out_specs=tspec(d),
        out_shape=jax.ShapeDtypeStruct((b, s, d), F32),
        scratch_shapes=[pltpu.VMEM((8 + tm, 2 * D_FF), F32), pltpu.VMEM((tm, D_FF), BF16)],
        compiler_params=pltpu.CompilerParams(
            dimension_semantics=("parallel", "arbitrary"), vmem_limit_bytes=VMEM_LIMIT),
        name="post_ffn",
    )(x, oa, ob, oc, mod, *consts, final_g)


def _rot_half_cols(w):
    half = MLA_ROPE // 2
    return jnp.concatenate([-w[..., half:], w[..., :half]], axis=-1)


def _stacked_weights(p):
    depth, d, _ = p["w_in"].shape
    splits = (MLA_Q_LORA, MLA_KV_LORA, MLA_ROPE, D_SG, D_SG, D_GQK, D_GQK, D_GLA, GLA_GATE_RANK, D_GLA)
    offs = [0]
    for n in splits:
        offs.append(offs[-1] + n)
    w_in = p["w_in"]
    col = lambda i: w_in[..., offs[i]:offs[i + 1]]
    win = jnp.concatenate([
        w_in[..., :offs[2]],
        col(8), col(2), _rot_half_cols(col(2)), jnp.zeros((depth, d, LANES - MISC_KRR - MLA_ROPE), F32),
        w_in[..., offs[3]:offs[8]], col(9)], axis=-1).astype(BF16)
    assert win.shape[-1] == Z_COLS

    keep3 = ((0, 0), (0, 0), (0, 0))
    wq = p["mla_w_uq"].astype(BF16).reshape(depth, MLA_Q_LORA, MLA_HEADS, MLA_NOPE + MLA_ROPE)
    nope, rope = wq[..., :MLA_NOPE], wq[..., MLA_NOPE:]
    pad_cols = jnp.zeros(rope.shape[:-1] + (HEAD_SLOT - MLA_NOPE - MLA_ROPE,), BF16)
    odd_head = (jnp.arange(MLA_HEADS) % 2 == 1)[None, None, :, None]
    wuq = jnp.where(odd_head, jnp.concatenate([rope, pad_cols, nope], axis=-1),
                    jnp.concatenate([nope, rope, pad_cols], axis=-1))
    wuq = jnp.swapaxes(wuq.reshape(depth, MLA_Q_LORA, MLA_HEADS * HEAD_SLOT), 1, 2)
    wuqr = _rot_half_cols(rope).reshape(depth, MLA_Q_LORA, MLA_HEADS * MLA_ROPE)
    wuqr = jnp.swapaxes(wuqr, 1, 2)
    wkv = p["mla_w_ukv"].astype(BF16).reshape(depth, MLA_KV_LORA, MLA_HEADS, MLA_NOPE + MLA_V)
    wuk = wkv[..., :MLA_NOPE].reshape(depth, MLA_KV_LORA, MLA_HEADS * MLA_NOPE)
    wuv = jnp.pad(wkv[..., MLA_NOPE:], keep3 + ((0, VT_ROWS - MLA_V),))
    wuv = jnp.swapaxes(wuv.reshape(depth, MLA_KV_LORA, MLA_HEADS * VT_ROWS), 1, 2)

    bs = jnp.repeat(jnp.swapaxes(p["sg_b_s"], 1, 2), SG_HEAD_DIM, axis=2)
    wg2 = jnp.pad(p["gla_w_gate2"].astype(BF16), ((0, 0), (0, LANES - GLA_GATE_RANK), (0, 0)))
    row = lambda a: a.reshape(depth, 1, -1)
    return {
        "ng1": row(p["norm_mix_g"]), "win": win,
        "qg": row(p["mla_q_norm_g"]), "kvg": row(p["mla_kv_norm_g"]),
        "wuq": wuq, "wuqr": wuqr, "wuk": wuk, "wuv": wuv,
        "lng": row(p["sg_ln_g"]), "lnb": row(p["sg_ln_b"]), "ws": p["sg_w_s"], "bs": bs,
        "wg2": wg2, "bg": row(p["gla_b_gate"]),
        "gng": row(jnp.tile(p["gla_norm_g"], (1, GLA_HEADS))),
        "wo": p["w_out"].astype(BF16), "ng2": row(p["norm_ffn_g"]),
        "wup": p["ffn_w_up"].astype(BF16), "cw": p["ffn_conv_w"],
        "cb": row(p["ffn_conv_b"]), "wdn": p["ffn_w_down"].astype(BF16),
    }


def kernel(x, c, positions, mod_w, mod_b, norm_mix_g, norm_ffn_g, w_in, mla_q_norm_g, mla_kv_norm_g, mla_w_uq, mla_w_ukv, sg_ln_g, sg_ln_b, sg_w_s, sg_b_s, gla_w_gate2, gla_b_gate, gla_norm_g, w_out, ffn_w_up, ffn_conv_w, ffn_conv_b, ffn_w_down, final_norm_g):
    p = dict(norm_mix_g=norm_mix_g, norm_ffn_g=norm_ffn_g, w_in=w_in, mla_q_norm_g=mla_q_norm_g,
             mla_kv_norm_g=mla_kv_norm_g, mla_w_uq=mla_w_uq, mla_w_ukv=mla_w_ukv, sg_ln_g=sg_ln_g,
             sg_ln_b=sg_ln_b, sg_w_s=sg_w_s, sg_b_s=sg_b_s, gla_w_gate2=gla_w_gate2,
             gla_b_gate=gla_b_gate, gla_norm_g=gla_norm_g, w_out=w_out, ffn_w_up=ffn_w_up,
             ffn_conv_w=ffn_conv_w, ffn_conv_b=ffn_conv_b, ffn_w_down=ffn_w_down)
    depth = mod_w.shape[0]
    b = x.shape[0]
    mod = _modulation(c, mod_w, mod_b).reshape(depth, b, 6, D_MODEL)
    tables = _rope_tables(positions)
    fg = final_norm_g.reshape(1, -1)
    w = _stacked_weights(p)
    for l in range(depth):
        q, k, v, o_sg, gq, gk, gla, gv, gr = _pre_call(x, mod, w, tables, l)
        o_mla = _attn_call(q, k, v)
        o_gla = _gla_call(gq, gk, gla, gv, gr, w["gng"], l)
        x = _post_call(x, o_mla, o_sg, o_gla, mod, w, fg, l, final=(l == depth - 1))
    return x
```

```python
import functools

import jax
import jax.numpy as jnp
from jax import lax
from jax.experimental import pallas as pl
from jax.experimental.pallas import tpu as pltpu

D_MODEL = 1024
MLA_HEADS = 8
MLA_NOPE = 64
MLA_ROPE = 32
MLA_V = 64
MLA_Q_LORA = 256
MLA_KV_LORA = 128
ROPE_THETA = 10000.0
SG_HEADS = 4
SG_HEAD_DIM = 64
SG_CHUNK = 128
GLA_HEADS = 4
GLA_DK = 32
GLA_DV = 64
GLA_GATE_RANK = 16
GLA_GATE_TAU = 16.0
GLA_CHUNK = 64
GLA_SUB = 8
D_FF = 2816
CONV_WIDTH = 3
EPS = 1e-6

D_MLA = MLA_HEADS * MLA_V
D_SG = SG_HEADS * SG_HEAD_DIM
D_GLA = GLA_HEADS * GLA_DV
D_GQK = GLA_HEADS * GLA_DK

LANES = 128
SUBLANE_ROWS = 8
HEAD_SLOT = LANES
ROPE_LO = MLA_NOPE
VT_ROWS = MLA_V + 16
VMEM_LIMIT = 56 * 1024 * 1024

Z_CQ = 0
Z_CKV = Z_CQ + MLA_Q_LORA
Z_MISC = Z_CKV + MLA_KV_LORA
Z_SU = Z_MISC + LANES
Z_SV = Z_SU + D_SG
Z_GQ = Z_SV + D_SG
Z_GK = Z_GQ + D_GQK
Z_GV = Z_GK + D_GQK
Z_GR = Z_GV + D_GLA
Z_COLS = Z_GR + D_GLA
MISC_KR = GLA_GATE_RANK
MISC_KRR = MISC_KR + MLA_ROPE
ODD_NOPE_LO = HEAD_SLOT - MLA_NOPE

TM_PRE = 1024
TQ_ATTN = 512
TG_GLA = 1024
ATTN_LOOKAHEAD = 2
TM_POST = 512
FFN_CHUNK = 256

F32 = jnp.float32
BF16 = jnp.bfloat16
NEG_BIG = -1e30
LOG2_E = 1.4426950408889634


def _dot(a, b):
    return jnp.dot(a, b, preferred_element_type=F32)


def _dot_nt(a, b):
    return lax.dot_general(a, b, (((1,), (1,)), ((), ())), preferred_element_type=F32)


def _dot_tn(a, b):
    return lax.dot_general(a, b, (((0,), (0,)), ((), ())), preferred_element_type=F32)


def _sigmoid(x):
    return 1.0 / (1.0 + jnp.exp(-x))


def _gelu_tanh(x):
    return 0.5 * x * (1.0 + jnp.tanh(0.7978845608028654 * (x + 0.044715 * (x * x * x))))


def _log_sigmoid(x):
    return jnp.minimum(x, 0.0) - jnp.log1p(jnp.exp(-jnp.abs(x)))


def _rms(x, g):
    return x * lax.rsqrt(jnp.mean(x * x, axis=-1, keepdims=True) + EPS) * g


def _const_spec(shape):
    zeros = (0,) * len(shape)
    return pl.BlockSpec(shape, lambda *_: zeros, pipeline_mode=pl.Buffered(1))


def _layer_spec(shape, l):
    idx = (l,) + (0,) * (len(shape) - 1)
    return pl.BlockSpec((1,) + tuple(shape[1:]), lambda *_: idx, pipeline_mode=pl.Buffered(1))


def _mod_spec(d, l):
    return pl.BlockSpec((1, 1, 6, d), lambda i, j: (l, i, 0, 0))


def _mod_kernel(c_ref, w_ref, b_ref, o_ref):
    c = c_ref[...]
    ca = c * _sigmoid(c)
    nb = ca.shape[0]
    a_hi = ca.astype(BF16)
    a_lo = (ca - a_hi.astype(F32)).astype(BF16)
    w = w_ref[0]
    w_hi = w.astype(BF16)
    w_lo = (w - w_hi.astype(F32)).astype(BF16)
    both = _dot(jnp.concatenate([a_hi, a_lo], axis=0), w_hi)
    o_ref[0] = both[0:nb] + both[nb:2 * nb] + _dot(a_hi, w_lo) + b_ref[0]


def _modulation(c, mod_w, mod_b):
    depth, d, n = mod_w.shape
    b = c.shape[0]
    tn = 2048
    return pl.pallas_call(
        _mod_kernel,
        grid=(depth, n // tn),
        in_specs=[
            pl.BlockSpec((b, d), lambda l, j: (0, 0)),
            pl.BlockSpec((1, d, tn), lambda l, j: (l, 0, j)),
            pl.BlockSpec((1, 1, tn), lambda l, j: (l, 0, j)),
        ],
        out_specs=pl.BlockSpec((1, b, tn), lambda l, j: (l, 0, j)),
        out_shape=jax.ShapeDtypeStruct((depth, b, n), F32),
        compiler_params=pltpu.CompilerParams(vmem_limit_bytes=VMEM_LIMIT),
        name="modulation",
    )(c, mod_w, mod_b.reshape(depth, 1, n))


def _rope_kernel(post_ref, invft_ref, cos_ref, sin_ref, cost_ref, sint_ref):
    angt = invft_ref[...] * post_ref[0]
    cost = jnp.cos(angt)
    sint = jnp.sin(angt)
    cost_ref[0] = cost
    sint_ref[0] = sint
    tm = angt.shape[1]
    zrows = lambda n: jnp.zeros((n, tm), F32)
    cos_ref[0] = jnp.concatenate([zrows(MISC_KR), cost, zrows(LANES - MISC_KR - MLA_ROPE)], axis=0).T
    sin_ref[0] = jnp.concatenate([zrows(MISC_KRR), sint, zrows(LANES - MISC_KRR - MLA_ROPE)], axis=0).T


def _rope_tables(positions):
    b, s = positions.shape
    tm = 512
    inv_freq = ROPE_THETA ** (-jnp.arange(0, MLA_ROPE, 2, dtype=F32) / MLA_ROPE)
    inv2 = jnp.concatenate([inv_freq, inv_freq])
    posf = positions.astype(F32)
    out = jax.ShapeDtypeStruct((b, s, LANES), F32)
    outt = jax.ShapeDtypeStruct((b, MLA_ROPE, s), F32)
    return pl.pallas_call(
        _rope_kernel,
        grid=(b, s // tm),
        in_specs=[pl.BlockSpec((1, 1, tm), lambda i, j: (i, 0, j)),
                  pl.BlockSpec((MLA_ROPE, 1), lambda i, j: (0, 0))],
        out_specs=[pl.BlockSpec((1, tm, LANES), lambda i, j: (i, j, 0))] * 2
                  + [pl.BlockSpec((1, MLA_ROPE, tm), lambda i, j: (i, 0, j))] * 2,
        out_shape=[out, out, outt, outt],
        name="rope_tables",
    )(posf.reshape(b, 1, s), inv2.reshape(MLA_ROPE, 1))


def _pre_kernel(x_ref, mod_ref, ng_ref, win_ref, qg_ref, kvg_ref, wuq_ref, wuqr_ref, wuk_ref,
                wuv_ref, cos_ref, sin_ref, cost_ref, sint_ref, lng_ref, lnb_ref, ws_ref, bs_ref,
                wg2_ref, bg_ref,
                q_out, k_out, v_out, sg_out, gq_out, gk_out, gla_out, gv_out, gr_out):
    tm = x_ref.shape[1]
    x = x_ref[0]
    sh1 = mod_ref[0, 0, 0:1, :]
    sc1 = mod_ref[0, 0, 1:2, :]
    h = _rms(x, ng_ref[0]) * (1.0 + sc1) + sh1
    z = _dot(h.astype(BF16), win_ref[0])

    cq = _rms(z[:, Z_CQ:Z_CQ + MLA_Q_LORA], qg_ref[0]).astype(BF16)
    ckv = _rms(z[:, Z_CKV:Z_CKV + MLA_KV_LORA], kvg_ref[0]).astype(BF16)
    qat = _dot_nt(wuq_ref[0], cq)
    qrt = _dot_nt(wuqr_ref[0], cq)
    cost = cost_ref[0]
    sint = sint_ref[0]
    scale = float(MLA_NOPE + MLA_ROPE) ** -0.5 * LOG2_E
    zero_rows = jnp.zeros((HEAD_SLOT - MLA_NOPE - MLA_ROPE, tm), BF16)
    for hd in range(MLA_HEADS):
        r0 = hd * HEAD_SLOT
        nope0, rope0 = (r0, r0 + ROPE_LO) if hd % 2 == 0 else (r0 + ODD_NOPE_LO, r0)
        rot = qrt[hd * MLA_ROPE:(hd + 1) * MLA_ROPE]
        q_out[0, nope0:nope0 + MLA_NOPE, :] = (qat[nope0:nope0 + MLA_NOPE] * scale).astype(BF16)
        q_out[0, rope0:rope0 + MLA_ROPE, :] = (
            (qat[rope0:rope0 + MLA_ROPE] * cost + rot * sint) * scale).astype(BF16)
        q_out[0, rope0 + MLA_ROPE:rope0 + MLA_ROPE + zero_rows.shape[0], :] = zero_rows
    misc = z[:, Z_MISC:Z_MISC + LANES]
    kc = misc * cos_ref[0]
    ks = misc * sin_ref[0]
    kr_even = pltpu.roll(kc, ROPE_LO - MISC_KR, 1) + pltpu.roll(ks, ROPE_LO - MISC_KRR, 1)
    kr_odd = pltpu.roll(kc, LANES - MISC_KR, 1) + pltpu.roll(ks, LANES - MISC_KRR, 1)
    kn = _dot(ckv, wuk_ref[0])
    klane = lax.broadcasted_iota(jnp.int32, (tm, LANES), 1)
    for hd in range(MLA_HEADS):
        pair = kn[:, (hd // 2) * LANES:(hd // 2 + 1) * LANES]
        slot = (jnp.where(klane < MLA_NOPE, pair, kr_even) if hd % 2 == 0
                else jnp.where(klane >= ODD_NOPE_LO, pair, kr_odd))
        k_out[0, :, hd * HEAD_SLOT:(hd + 1) * HEAD_SLOT] = slot.astype(BF16)
    vt = _dot_nt(wuv_ref[0], ckv)
    vrow = lax.broadcasted_iota(jnp.int32, vt.shape, 0)
    v_out[0] = jnp.where(vrow % VT_ROWS == MLA_V, 1.0, vt).astype(BF16)

    u = _gelu_tanh(z[:, Z_SU:Z_SU + D_SG])
    gv = _gelu_tanh(z[:, Z_SV:Z_SV + D_SG])
    mu = jnp.mean(gv, axis=-1, keepdims=True)
    dv = gv - mu
    var = jnp.mean(dv * dv, axis=-1, keepdims=True)
    vn = (dv * lax.rsqrt(var + EPS) * lng_ref[0] + lnb_ref[0]).astype(BF16)
    trow = lax.broadcasted_iota(jnp.int32, (SG_CHUNK, SG_CHUNK), 0)
    tcol = lax.broadcasted_iota(jnp.int32, (SG_CHUNK, SG_CHUNK), 1)
    wmix = [jnp.where(trow >= tcol, ws_ref[0, i], 0.0).astype(BF16) for i in range(SG_HEADS)]
    lane = lax.broadcasted_iota(jnp.int32, (SG_CHUNK, LANES), 1)
    lo_half = lane < SG_HEAD_DIM
    nchunk = tm // SG_CHUNK
    for p in range(D_SG // LANES):
        cols = slice(p * LANES, (p + 1) * LANES)
        vp = jnp.concatenate([vn[c * SG_CHUNK:(c + 1) * SG_CHUNK, cols] for c in range(nchunk)], axis=1)
        m_lo = _dot(wmix[2 * p], vp)
        m_hi = _dot(wmix[2 * p + 1], vp)
        for c in range(nchunk):
            rows = slice(c * SG_CHUNK, (c + 1) * SG_CHUNK)
            blk = slice(c * LANES, (c + 1) * LANES)
            mixed = jnp.where(lo_half, m_lo[:, blk], m_hi[:, blk])
            sg_out[0, rows, cols] = (u[rows, cols] * (mixed + bs_ref[0, :, cols])).astype(BF16)

    gq_out[0] = z[:, Z_GQ:Z_GQ + D_GQK] * (float(GLA_DK) ** -0.5)
    gk_out[0] = z[:, Z_GK:Z_GK + D_GQK]
    gv_out[0] = z[:, Z_GV:Z_GV + D_GLA]
    gate = _dot(misc.astype(BF16), wg2_ref[0]) + bg_ref[0]
    gla_out[0] = _log_sigmoid(gate) * (1.0 / GLA_GATE_TAU)
    gr = z[:, Z_GR:Z_GR + D_GLA]
    gr_out[0] = gr * _sigmoid(gr)


def _pre_call(x, mod, w, tables, l):
    b, s, d = x.shape
    tm = TM_PRE
    tok = lambda n, dt: jax.ShapeDtypeStruct((b, s, n), dt)
    tspec = lambda n: pl.BlockSpec((1, tm, n), lambda i, j: (i, j, 0))
    tspec_t = lambda n: pl.BlockSpec((1, n, tm), lambda i, j: (i, 0, j))
    consts = [w["ng1"], w["win"], w["qg"], w["kvg"], w["wuq"], w["wuqr"], w["wuk"], w["wuv"]]
    consts2 = [w["lng"], w["lnb"], w["ws"], w["bs"], w["wg2"], w["bg"]]
    in_specs = ([tspec(d), _mod_spec(d, l)]
                + [_layer_spec(a.shape, l) for a in consts]
                + [tspec(LANES), tspec(LANES), tspec_t(MLA_ROPE), tspec_t(MLA_ROPE)]
                + [_layer_spec(a.shape, l) for a in consts2])
    outs = [(MLA_HEADS * HEAD_SLOT, BF16), (MLA_HEADS * HEAD_SLOT, BF16), (D_MLA, BF16),
            (D_SG, BF16), (D_GQK, F32), (D_GQK, F32), (D_GQK, F32), (D_GLA, F32), (D_GLA, F32)]
    out_specs = [tspec(n) for n, _ in outs]
    out_shape = [tok(n, dt) for n, dt in outs]
    out_specs[0] = tspec_t(MLA_HEADS * HEAD_SLOT)
    out_shape[0] = jax.ShapeDtypeStruct((b, MLA_HEADS * HEAD_SLOT, s), BF16)
    out_specs[2] = tspec_t(MLA_HEADS * VT_ROWS)
    out_shape[2] = jax.ShapeDtypeStruct((b, MLA_HEADS * VT_ROWS, s), BF16)
    return pl.pallas_call(
        _pre_kernel,
        grid=(b, s // tm),
        in_specs=in_specs,
        out_specs=out_specs,
        out_shape=out_shape,
        compiler_params=pltpu.CompilerParams(
            dimension_semantics=("parallel", "parallel"), vmem_limit_bytes=VMEM_LIMIT),
        name="pre_mix",
    )(x, mod, *consts, *tables, *consts2)


def _attn_kernel(q_ref, k_ref, vt_ref, o_ref, m_s, acc_s, st_s):
    tq = q_ref.shape[2]
    tk = tq
    qi = pl.program_id(1)
    hk = tk // 2
    tri = (lax.broadcasted_iota(jnp.int32, (hk, hk), 0)
           <= lax.broadcasted_iota(jnp.int32, (hk, hk), 1))

    m_s[...] = jnp.full(m_s.shape, NEG_BIG, F32)
    acc_s[...] = jnp.zeros(acc_s.shape, F32)

    def run_heads(scores, softmax_pv):
        for hd in range(ATTN_LOOKAHEAD):
            scores(hd)
        for hd in range(MLA_HEADS):
            if hd + ATTN_LOOKAHEAD < MLA_HEADS:
                scores(hd + ATTN_LOOKAHEAD)
            softmax_pv(hd)

    def full_step(j):
        off = pl.multiple_of(j * tk, tk)

        def scores(hd):
            hcols = slice(hd * HEAD_SLOT, (hd + 1) * HEAD_SLOT)
            st_s[hd] = _dot(k_ref[0, pl.ds(off, tk), hcols], q_ref[0, hcols, :])

        def softmax_pv(hd):
            m_old = m_s[hd]
            m_new = jnp.maximum(m_old, jnp.max(st_s[hd], axis=0, keepdims=True))
            alpha = jnp.exp2(m_old - m_new)
            m_s[hd] = m_new
            e = jnp.exp2((st_s[hd] - m_new).astype(BF16))
            vrows = slice(hd * VT_ROWS, (hd + 1) * VT_ROWS)
            pv = _dot(vt_ref[0, vrows, pl.ds(off, tk)], e)
            acc_s[vrows, :] = alpha * acc_s[vrows, :] + pv

        run_heads(scores, softmax_pv)

    def diag_step():
        off_a = pl.multiple_of(qi * tk, tk)
        off_b = pl.multiple_of(qi * tk + hk, hk)

        def scores(hd):
            hcols = slice(hd * HEAD_SLOT, (hd + 1) * HEAD_SLOT)
            sa = _dot(k_ref[0, pl.ds(off_a, hk), hcols], q_ref[0, hcols, :])
            sb = _dot(k_ref[0, pl.ds(off_b, hk), hcols], q_ref[0, hcols, hk:])
            st_s[hd, 0:hk, 0:hk] = jnp.where(tri, sa[:, 0:hk], NEG_BIG)
            st_s[hd, 0:hk, hk:] = sa[:, hk:]
            st_s[hd, hk:, hk:] = jnp.where(tri, sb, NEG_BIG)

        def softmax_pv(hd):
            m_old = m_s[hd]
            ma = jnp.max(st_s[hd, 0:hk, :], axis=0, keepdims=True)
            mb = jnp.max(st_s[hd, hk:, hk:], axis=0, keepdims=True)
            m_new = jnp.maximum(m_old, jnp.concatenate(
                [ma[:, 0:hk], jnp.maximum(ma[:, hk:], mb)], axis=1))
            alpha = jnp.exp2(m_old - m_new)
            m_s[hd] = m_new
            ea = jnp.exp2((st_s[hd, 0:hk, :] - m_new).astype(BF16))
            eb = jnp.exp2((st_s[hd, hk:, hk:] - m_new[:, hk:]).astype(BF16))
            vrows = slice(hd * VT_ROWS, (hd + 1) * VT_ROWS)
            pva = _dot(vt_ref[0, vrows, pl.ds(off_a, hk)], ea)
            pvb = _dot(vt_ref[0, vrows, pl.ds(off_b, hk)], eb)
            acc_s[vrows, 0:hk] = alpha[:, 0:hk] * acc_s[vrows, 0:hk] + pva[:, 0:hk]
            acc_s[vrows, hk:] = alpha[:, hk:] * acc_s[vrows, hk:] + pva[:, hk:] + pvb

        run_heads(scores, softmax_pv)

    def body(j, carry):
        full_step(j)
        return carry

    lax.fori_loop(0, qi, body, 0)
    diag_step()

    for p in range(MLA_HEADS // 2):
        halves = []
        for hd in (2 * p, 2 * p + 1):
            r0 = hd * VT_ROWS
            halves.append(acc_s[r0:r0 + MLA_V, :] * (1.0 / acc_s[r0 + MLA_V:r0 + MLA_V + 1, :]))
        ot = jnp.concatenate(halves, axis=0)
        o_ref[0, :, p * LANES:(p + 1) * LANES] = ot.T.astype(BF16)


def _attn_call(qt, k, vt):
    b, s, _ = k.shape
    tq = TQ_ATTN
    return pl.pallas_call(
        _attn_kernel,
        grid=(b, s // tq),
        in_specs=[pl.BlockSpec((1, MLA_HEADS * HEAD_SLOT, tq), lambda i, j: (i, 0, j)),
                  pl.BlockSpec((1, s, MLA_HEADS * HEAD_SLOT), lambda i, j: (i, 0, 0)),
                  pl.BlockSpec((1, MLA_HEADS * VT_ROWS, s), lambda i, j: (i, 0, 0))],
        out_specs=pl.BlockSpec((1, tq, D_MLA), lambda i, j: (i, j, 0)),
        out_shape=jax.ShapeDtypeStruct((b, s, D_MLA), BF16),
        scratch_shapes=[pltpu.VMEM((MLA_HEADS, 1, tq), F32),
                        pltpu.VMEM((MLA_HEADS * VT_ROWS, tq), F32),
                        pltpu.VMEM((MLA_HEADS, tq, tq), F32)],
        compiler_params=pltpu.CompilerParams(
            dimension_semantics=("parallel", "arbitrary"), vmem_limit_bytes=VMEM_LIMIT),
        name="mla_attention",
    )(qt, k, vt)


def _gla_kernel(q_ref, k_ref, la_ref, v_ref, r_ref, g_ref, o_ref, st_ref):
    tg = q_ref.shape[1]
    ch = GLA_CHUNK
    sb = GLA_SUB

    @pl.when(pl.program_id(1) == 0)
    def _():
        st_ref[...] = jnp.zeros_like(st_ref)

    q = q_ref[0]
    k = k_ref[0]
    v = v_ref[0]
    ti = lax.broadcasted_iota(jnp.int32, (ch, ch), 0)
    tj = lax.broadcasted_iota(jnp.int32, (ch, ch), 1)
    tri = jnp.where(tj <= ti, 1.0, 0.0).astype(BF16)
    la = la_ref[0] * LOG2_E
    la_hi = la.astype(BF16)
    la_r1 = la - la_hi.astype(F32)
    la_mid = la_r1.astype(BF16)
    la_lo = (la_r1 - la_mid.astype(F32)).astype(BF16)
    b = jnp.concatenate(
        [_dot(tri, la_hi[c * ch:(c + 1) * ch]) + _dot(tri, la_mid[c * ch:(c + 1) * ch])
         + _dot(tri, la_lo[c * ch:(c + 1) * ch]) for c in range(tg // ch)], axis=0)

    er = lax.broadcasted_iota(jnp.int32, (D_GQK, D_GLA), 0)
    ec = lax.broadcasted_iota(jnp.int32, (D_GQK, D_GLA), 1)
    head_sum = jnp.where((er // GLA_DK) == (ec // GLA_DV), 1.0, 0.0).astype(BF16)

    rowmod = lax.broadcasted_iota(jnp.int32, (tg, D_GQK), 0) % sb
    acc = _dot((q * k).astype(BF16), head_sum) * v
    assert sb == SUBLANE_ROWS
    in_tile_shift = lambda a, s: pltpu.roll(a.reshape(tg // sb, sb, a.shape[1]), s, 1).reshape(a.shape)
    for s in range(1, sb):
        ks = in_tile_shift(k, s)
        bs = in_tile_shift(b, s)
        vs = in_tile_shift(v, s)
        p = jnp.where(rowmod >= s, q * ks * jnp.exp2(b - bs), 0.0)
        acc = acc + _dot(p.astype(BF16), head_sum) * vs

    sr = lax.broadcasted_iota(jnp.int32, (D_GLA, D_GQK), 0)
    sc = lax.broadcasted_iota(jnp.int32, (D_GLA, D_GQK), 1)
    same_head = (sr // GLA_DV) == (sc // GLA_DK)
    gr = lax.broadcasted_iota(jnp.int32, (D_GLA, D_GLA), 0)
    gc = lax.broadcasted_iota(jnp.int32, (D_GLA, D_GLA), 1)
    head_mean = jnp.where((gr // GLA_DV) == (gc // GLA_DV), 1.0 / GLA_DV, 0.0).astype(BF16)
    klane = lax.broadcasted_iota(jnp.int32, (sb, D_GQK), 1) // GLA_DK
    vlane = lax.broadcasted_iota(jnp.int32, (sb, D_GLA), 1) // GLA_DV
    krow = lax.broadcasted_iota(jnp.int32, (ch, D_GQK), 0)
    nch = tg // ch
    chunk = lambda a, c: a[c * ch:(c + 1) * ch]
    vb = v.astype(BF16)

    scores = []
    for c in range(nch):
        bc, qc, kc = chunk(b, c), chunk(q, c), chunk(k, c)
        for blk in range(1, ch // sb):
            lo_r = blk * sb
            r = bc[lo_r:lo_r + 1, :]
            qt = qc[lo_r:lo_r + sb] * jnp.exp2(bc[lo_r:lo_r + sb] - r)
            kt = (kc * jnp.exp2(jnp.where(krow < lo_r, r - bc, NEG_BIG))).astype(BF16)
            qexp = jnp.concatenate([jnp.where(klane == hd, qt, 0.0) for hd in range(GLA_HEADS)], axis=0)
            scores.append(_dot_nt(qexp.astype(BF16), kt))
    upds = []
    for c in range(nch):
        bc = chunk(b, c)
        kd = (chunk(k, c) * jnp.exp2(bc[ch - 1:ch, :] - bc)).astype(BF16)
        upds.append(jnp.where(same_head, _dot_tn(chunk(vb, c), kd), 0.0))
    o_sub = []
    for c in range(nch):
        pieces = [jnp.zeros((sb, D_GLA), F32)]
        for blk in range(1, ch // sb):
            oh = _dot(scores[c * (ch // sb - 1) + blk - 1].astype(BF16), chunk(vb, c))
            piece = jnp.where(vlane == 0, oh[0:sb], 0.0)
            for hd in range(1, GLA_HEADS):
                piece = piece + jnp.where(vlane == hd, oh[hd * sb:(hd + 1) * sb], 0.0)
            pieces.append(piece)
        o_sub.append(jnp.concatenate(pieces, axis=0))
    states = []
    st = st_ref[...]
    for c in range(nch):
        states.append(st.astype(BF16))
        st = st * jnp.exp2(chunk(b, c)[ch - 1:ch, :]) + upds[c]
    st_ref[...] = st
    o = jnp.concatenate(
        [chunk(acc, c) + o_sub[c]
         + _dot_nt((chunk(q, c) * jnp.exp2(chunk(b, c))).astype(BF16), states[c]) for c in range(nch)], axis=0)
    o2 = o * o
    hi = o2.astype(BF16)
    lo = (o2 - hi.astype(F32)).astype(BF16)
    ms = _dot(hi, head_mean) + _dot(lo, head_mean)
    o_ref[0] = (o * lax.rsqrt(ms + EPS) * g_ref[0] * r_ref[0]).astype(BF16)


def _gla_call(gq, gk, gla, gv, gr, g256, l):
    b, s, _ = gq.shape
    tg = TG_GLA
    tspec = lambda n: pl.BlockSpec((1, tg, n), lambda i, j: (i, j, 0))
    return pl.pallas_call(
        _gla_kernel,
        grid=(b, s // tg),
        in_specs=[tspec(D_GQK), tspec(D_GQK), tspec(D_GQK), tspec(D_GLA), tspec(D_GLA),
                  _layer_spec(g256.shape, l)],
        out_specs=tspec(D_GLA),
        out_shape=jax.ShapeDtypeStruct((b, s, D_GLA), BF16),
        scratch_shapes=[pltpu.VMEM((D_GLA, D_GQK), F32)],
        compiler_params=pltpu.CompilerParams(
            dimension_semantics=("parallel", "arbitrary"), vmem_limit_bytes=VMEM_LIMIT),
        name="gla",
    )(gq, gk, gla, gv, gr, g256)


def _post_kernel(x_ref, oa_ref, ob_ref, oc_ref, mod_ref, wo_ref, ng_ref, wup_ref, cw_ref, cb_ref,
                 wdn_ref, fg_ref, o_ref, zbuf, act, *, final):
    tm = x_ref.shape[1]
    halo = 8
    g1 = mod_ref[0, 0, 2:3, :]
    sh2 = mod_ref[0, 0, 3:4, :]
    sc2 = mod_ref[0, 0, 4:5, :]
    g2 = mod_ref[0, 0, 5:6, :]
    mix = (_dot(oa_ref[0], wo_ref[0, 0:D_MLA, :])
           + _dot(ob_ref[0], wo_ref[0, D_MLA:D_MLA + D_SG, :])
           + _dot(oc_ref[0], wo_ref[0, D_MLA + D_SG:, :]))
    x1 = x_ref[0] + g1 * mix
    h2 = _rms(x1, ng_ref[0]) * (1.0 + sc2) + sh2

    @pl.when(pl.program_id(1) == 0)
    def _():
        zbuf[0:halo, :] = jnp.zeros((halo, 2 * D_FF), F32)

    zbuf[halo:halo + tm, :] = _dot(h2.astype(BF16), wup_ref[0])
    cc = FFN_CHUNK
    for j in range(D_FF // cc):
        parts = []
        for cols in (slice(j * cc, (j + 1) * cc), slice(D_FF + j * cc, D_FF + (j + 1) * cc)):
            zc = cb_ref[0, :, cols] + cw_ref[0, 2:3, cols] * zbuf[halo:halo + tm, cols]
            zc = zc + cw_ref[0, 0:1, cols] * zbuf[halo - 2:halo - 2 + tm, cols]
            zc = zc + cw_ref[0, 1:2, cols] * zbuf[halo - 1:halo - 1 + tm, cols]
            parts.append(zc)
        val, gate = parts
        act[:, j * cc:(j + 1) * cc] = (gate * _sigmoid(gate) * val).astype(BF16)
    zbuf[halo - 2:halo, :] = zbuf[halo + tm - 2:halo + tm, :]
    x2 = x1 + g2 * _dot(act[...], wdn_ref[0])
    if final:
        x2 = _rms(x2, fg_ref[...])
    o_ref[0] = x2


def _post_call(x, oa, ob, oc, mod, w, final_g, l, final):
    b, s, d = x.shape
    tm = TM_POST
    tspec = lambda n: pl.BlockSpec((1, tm, n), lambda i, j: (i, j, 0))
    consts = [w["wo"], w["ng2"], w["wup"], w["cw"], w["cb"], w["wdn"]]
    return pl.pallas_call(
        functools.partial(_post_kernel, final=final),
        grid=(b, s // tm),
        in_specs=[tspec(d), tspec(D_MLA), tspec(D_SG), tspec(D_GLA), _mod_spec(d, l)]
                 + [_layer_spec(a.shape, l) for a in consts] + [_const_spec(final_g.shape)],
        out_specs=tspec(d),
        out_shape=jax.ShapeDtypeStruct((b, s, d), F32),
        scratch_shapes=[pltpu.VMEM((8 + tm, 2 * D_FF), F32), pltpu.VMEM((tm, D_FF), BF16)],
        compiler_params=pltpu.CompilerParams(
            dimension_semantics=("parallel", "arbitrary"), vmem_limit_bytes=VMEM_LIMIT),
        name="post_ffn",
    )(x, oa, ob, oc, mod, *consts, final_g)


def _rot_half_cols(w):
    half = MLA_ROPE // 2
    return jnp.concatenate([-w[..., half:], w[..., :half]], axis=-1)


def _stacked_weights(p):
    depth, d, _ = p["w_in"].shape
    splits = (MLA_Q_LORA, MLA_KV_LORA, MLA_ROPE, D_SG, D_SG, D_GQK, D_GQK, D_GLA, GLA_GATE_RANK, D_GLA)
    offs = [0]
    for n in splits:
        offs.append(offs[-1] + n)
    w_in = p["w_in"]
    col = lambda i: w_in[..., offs[i]:offs[i + 1]]
    win = jnp.concatenate([
        w_in[..., :offs[2]],
        col(8), col(2), _rot_half_cols(col(2)), jnp.zeros((depth, d, LANES - MISC_KRR - MLA_ROPE), F32),
        w_in[..., offs[3]:offs[8]], col(9)], axis=-1).astype(BF16)
    assert win.shape[-1] == Z_COLS

    keep3 = ((0, 0), (0, 0), (0, 0))
    wq = p["mla_w_uq"].astype(BF16).reshape(depth, MLA_Q_LORA, MLA_HEADS, MLA_NOPE + MLA_ROPE)
    nope, rope = wq[..., :MLA_NOPE], wq[..., MLA_NOPE:]
    pad_cols = jnp.zeros(rope.shape[:-1] + (HEAD_SLOT - MLA_NOPE - MLA_ROPE,), BF16)
    odd_head = (jnp.arange(MLA_HEADS) % 2 == 1)[None, None, :, None]
    wuq = jnp.where(odd_head, jnp.concatenate([rope, pad_cols, nope], axis=-1),
                    jnp.concatenate([nope, rope, pad_cols], axis=-1))
    wuq = jnp.swapaxes(wuq.reshape(depth, MLA_Q_LORA, MLA_HEADS * HEAD_SLOT), 1, 2)
    wuqr = _rot_half_cols(rope).reshape(depth, MLA_Q_LORA, MLA_HEADS * MLA_ROPE)
    wuqr = jnp.swapaxes(wuqr, 1, 2)
    wkv = p["mla_w_ukv"].astype(BF16).reshape(depth, MLA_KV_LORA, MLA_HEADS, MLA_NOPE + MLA_V)
    wuk = wkv[..., :MLA_NOPE].reshape(depth, MLA_KV_LORA, MLA_HEADS * MLA_NOPE)
    wuv = jnp.pad(wkv[..., MLA_NOPE:], keep3 + ((0, VT_ROWS - MLA_V),))
    wuv = jnp.swapaxes(wuv.reshape(depth, MLA_KV_LORA, MLA_HEADS * VT_ROWS), 1, 2)

    bs = jnp.repeat(jnp.swapaxes(p["sg_b_s"], 1, 2), SG_HEAD_DIM, axis=2)
    wg2 = jnp.pad(p["gla_w_gate2"].astype(BF16), ((0, 0), (0, LANES - GLA_GATE_RANK), (0, 0)))
    row = lambda a: a.reshape(depth, 1, -1)
    return {
        "ng1": row(p["norm_mix_g"]), "win": win,
        "qg": row(p["mla_q_norm_g"]), "kvg": row(p["mla_kv_norm_g"]),
        "wuq": wuq, "wuqr": wuqr, "wuk": wuk, "wuv": wuv,
        "lng": row(p["sg_ln_g"]), "lnb": row(p["sg_ln_b"]), "ws": p["sg_w_s"], "bs": bs,
        "wg2": wg2, "bg": row(p["gla_b_gate"]),
        "gng": row(jnp.tile(p["gla_norm_g"], (1, GLA_HEADS))),
        "wo": p["w_out"].astype(BF16), "ng2": row(p["norm_ffn_g"]),
        "wup": p["ffn_w_up"].astype(BF16), "cw": p["ffn_conv_w"],
        "cb": row(p["ffn_conv_b"]), "wdn": p["ffn_w_down"].astype(BF16),
    }


def kernel(x, c, positions, mod_w, mod_b, norm_mix_g, norm_ffn_g, w_in, mla_q_norm_g, mla_kv_norm_g, mla_w_uq, mla_w_ukv, sg_ln_g, sg_ln_b, sg_w_s, sg_b_s, gla_w_gate2, gla_b_gate, gla_norm_g, w_out, ffn_w_up, ffn_conv_w, ffn_conv_b, ffn_w_down, final_norm_g):
    p = dict(norm_mix_g=norm_mix_g, norm_ffn_g=norm_ffn_g, w_in=w_in, mla_q_norm_g=mla_q_norm_g,
             mla_kv_norm_g=mla_kv_norm_g, mla_w_uq=mla_w_uq, mla_w_ukv=mla_w_ukv, sg_ln_g=sg_ln_g,
             sg_ln_b=sg_ln_b, sg_w_s=sg_w_s, sg_b_s=sg_b_s, gla_w_gate2=gla_w_gate2,
             gla_b_gate=gla_b_gate, gla_norm_g=gla_norm_g, w_out=w_out, ffn_w_up=ffn_w_up,
             ffn_conv_w=ffn_conv_w, ffn_conv_b=ffn_conv_b, ffn_w_down=ffn_w_down)
    depth = mod_w.shape[0]
    b = x.shape[0]
    mod = _modulation(c, mod_w, mod_b).reshape(depth, b, 6, D_MODEL)
    tables = _rope_tables(positions)
    fg = final_norm_g.reshape(1, -1)
    w = _stacked_weights(p)
    for l in range(depth):
        q, k, v, o_sg, gq, gk, gla, gv, gr = _pre_call(x, mod, w, tables, l)
        o_mla = _attn_call(q, k, v)
        o_gla = _gla_call(gq, gk, gla, gv, gr, w["gng"], l)
        x = _post_call(x, o_mla, o_sg, o_gla, mod, w, fg, l, final=(l == depth - 1))
    return x
```

```python
import functools

import jax
import jax.numpy as jnp
from jax import lax
from jax.experimental import pallas as pl
from jax.experimental.pallas import tpu as pltpu

D_MODEL = 1024
MLA_HEADS = 8
MLA_NOPE = 64
MLA_ROPE = 32
MLA_V = 64
MLA_Q_LORA = 256
MLA_KV_LORA = 128
ROPE_THETA = 10000.0
SG_HEADS = 4
SG_HEAD_DIM = 64
SG_CHUNK = 128
GLA_HEADS = 4
GLA_DK = 32
GLA_DV = 64
GLA_GATE_RANK = 16
GLA_GATE_TAU = 16.0
GLA_CHUNK = 64
GLA_SUB = 8
D_FF = 2816
CONV_WIDTH = 3
EPS = 1e-6

D_MLA = MLA_HEADS * MLA_V
D_SG = SG_HEADS * SG_HEAD_DIM
D_GLA = GLA_HEADS * GLA_DV
D_GQK = GLA_HEADS * GLA_DK

LANES = 128
SUBLANE_ROWS = 8
HEAD_SLOT = LANES
ROPE_LO = MLA_NOPE
VT_ROWS = MLA_V + 16
VMEM_LIMIT = 56 * 1024 * 1024

Z_CQ = 0
Z_CKV = Z_CQ + MLA_Q_LORA
Z_MISC = Z_CKV + MLA_KV_LORA
Z_SU = Z_MISC + LANES
Z_SV = Z_SU + D_SG
Z_GQ = Z_SV + D_SG
Z_GK = Z_GQ + D_GQK
Z_GV = Z_GK + D_GQK
Z_GR = Z_GV + D_GLA
Z_COLS = Z_GR + D_GLA
MISC_KR = GLA_GATE_RANK
MISC_KRR = MISC_KR + MLA_ROPE
ODD_NOPE_LO = HEAD_SLOT - MLA_NOPE

TM_PRE = 1024
TQ_ATTN = 512
TG_GLA = 2048
ATTN_LOOKAHEAD = 2
TM_POST = 512
FFN_CHUNK = 256

F32 = jnp.float32
BF16 = jnp.bfloat16
NEG_BIG = -1e30
LOG2_E = 1.4426950408889634


def _dot(a, b):
    return jnp.dot(a, b, preferred_element_type=F32)


def _dot_nt(a, b):
    return lax.dot_general(a, b, (((1,), (1,)), ((), ())), preferred_element_type=F32)


def _dot_tn(a, b):
    return lax.dot_general(a, b, (((0,), (0,)), ((), ())), preferred_element_type=F32)


def _sigmoid(x):
    return 1.0 / (1.0 + jnp.exp(-x))


def _gelu_tanh(x):
    return 0.5 * x * (1.0 + jnp.tanh(0.7978845608028654 * (x + 0.044715 * (x * x * x))))


def _log_sigmoid(x):
    return jnp.minimum(x, 0.0) - jnp.log1p(jnp.exp(-jnp.abs(x)))


def _rms(x, g):
    return x * lax.rsqrt(jnp.mean(x * x, axis=-1, keepdims=True) + EPS) * g


def _const_spec(shape):
    zeros = (0,) * len(shape)
    return pl.BlockSpec(shape, lambda *_: zeros, pipeline_mode=pl.Buffered(1))


def _layer_spec(shape, l):
    idx = (l,) + (0,) * (len(shape) - 1)
    return pl.BlockSpec((1,) + tuple(shape[1:]), lambda *_: idx, pipeline_mode=pl.Buffered(1))


def _mod_spec(d, l):
    return pl.BlockSpec((1, 1, 6, d), lambda i, j: (l, i, 0, 0))


def _mod_kernel(c_ref, w_ref, b_ref, o_ref):
    c = c_ref[...]
    ca = c * _sigmoid(c)
    nb = ca.shape[0]
    a_hi = ca.astype(BF16)
    a_lo = (ca - a_hi.astype(F32)).astype(BF16)
    w = w_ref[0]
    w_hi = w.astype(BF16)
    w_lo = (w - w_hi.astype(F32)).astype(BF16)
    both = _dot(jnp.concatenate([a_hi, a_lo], axis=0), w_hi)
    o_ref[0] = both[0:nb] + both[nb:2 * nb] + _dot(a_hi, w_lo) + b_ref[0]


def _modulation(c, mod_w, mod_b):
    depth, d, n = mod_w.shape
    b = c.shape[0]
    tn = 2048
    return pl.pallas_call(
        _mod_kernel,
        grid=(depth, n // tn),
        in_specs=[
            pl.BlockSpec((b, d), lambda l, j: (0, 0)),
            pl.BlockSpec((1, d, tn), lambda l, j: (l, 0, j)),
            pl.BlockSpec((1, 1, tn), lambda l, j: (l, 0, j)),
        ],
        out_specs=pl.BlockSpec((1, b, tn), lambda l, j: (l, 0, j)),
        out_shape=jax.ShapeDtypeStruct((depth, b, n), F32),
        compiler_params=pltpu.CompilerParams(vmem_limit_bytes=VMEM_LIMIT),
        name="modulation",
    )(c, mod_w, mod_b.reshape(depth, 1, n))


def _rope_kernel(post_ref, invft_ref, cos_ref, sin_ref, cost_ref, sint_ref):
    angt = invft_ref[...] * post_ref[0]
    cost = jnp.cos(angt)
    sint = jnp.sin(angt)
    cost_ref[0] = cost
    sint_ref[0] = sint
    tm = angt.shape[1]
    zrows = lambda n: jnp.zeros((n, tm), F32)
    cos_ref[0] = jnp.concatenate([zrows(MISC_KR), cost, zrows(LANES - MISC_KR - MLA_ROPE)], axis=0).T
    sin_ref[0] = jnp.concatenate([zrows(MISC_KRR), sint, zrows(LANES - MISC_KRR - MLA_ROPE)], axis=0).T


def _rope_tables(positions):
    b, s = positions.shape
    tm = 512
    inv_freq = ROPE_THETA ** (-jnp.arange(0, MLA_ROPE, 2, dtype=F32) / MLA_ROPE)
    inv2 = jnp.concatenate([inv_freq, inv_freq])
    posf = positions.astype(F32)
    out = jax.ShapeDtypeStruct((b, s, LANES), F32)
    outt = jax.ShapeDtypeStruct((b, MLA_ROPE, s), F32)
    return pl.pallas_call(
        _rope_kernel,
        grid=(b, s // tm),
        in_specs=[pl.BlockSpec((1, 1, tm), lambda i, j: (i, 0, j)),
                  pl.BlockSpec((MLA_ROPE, 1), lambda i, j: (0, 0))],
        out_specs=[pl.BlockSpec((1, tm, LANES), lambda i, j: (i, j, 0))] * 2
                  + [pl.BlockSpec((1, MLA_ROPE, tm), lambda i, j: (i, 0, j))] * 2,
        out_shape=[out, out, outt, outt],
        name="rope_tables",
    )(posf.reshape(b, 1, s), inv2.reshape(MLA_ROPE, 1))


def _pre_kernel(x_ref, mod_ref, ng_ref, win_ref, qg_ref, kvg_ref, wuq_ref, wuqr_ref, wuk_ref,
                wuv_ref, cos_ref, sin_ref, cost_ref, sint_ref, lng_ref, lnb_ref, ws_ref, bs_ref,
                wg2_ref, bg_ref,
                q_out, k_out, v_out, sg_out, gq_out, gk_out, gla_out, gv_out, gr_out):
    tm = x_ref.shape[1]
    x = x_ref[0]
    sh1 = mod_ref[0, 0, 0:1, :]
    sc1 = mod_ref[0, 0, 1:2, :]
    h = _rms(x, ng_ref[0]) * (1.0 + sc1) + sh1
    z = _dot(h.astype(BF16), win_ref[0])

    cq = _rms(z[:, Z_CQ:Z_CQ + MLA_Q_LORA], qg_ref[0]).astype(BF16)
    ckv = _rms(z[:, Z_CKV:Z_CKV + MLA_KV_LORA], kvg_ref[0]).astype(BF16)
    qat = _dot_nt(wuq_ref[0], cq)
    qrt = _dot_nt(wuqr_ref[0], cq)
    cost = cost_ref[0]
    sint = sint_ref[0]
    scale = float(MLA_NOPE + MLA_ROPE) ** -0.5 * LOG2_E
    zero_rows = jnp.zeros((HEAD_SLOT - MLA_NOPE - MLA_ROPE, tm), BF16)
    for hd in range(MLA_HEADS):
        r0 = hd * HEAD_SLOT
        nope0, rope0 = (r0, r0 + ROPE_LO) if hd % 2 == 0 else (r0 + ODD_NOPE_LO, r0)
        rot = qrt[hd * MLA_ROPE:(hd + 1) * MLA_ROPE]
        q_out[0, nope0:nope0 + MLA_NOPE, :] = (qat[nope0:nope0 + MLA_NOPE] * scale).astype(BF16)
        q_out[0, rope0:rope0 + MLA_ROPE, :] = (
            (qat[rope0:rope0 + MLA_ROPE] * cost + rot * sint) * scale).astype(BF16)
        q_out[0, rope0 + MLA_ROPE:rope0 + MLA_ROPE + zero_rows.shape[0], :] = zero_rows
    misc = z[:, Z_MISC:Z_MISC + LANES]
    kc = misc * cos_ref[0]
    ks = misc * sin_ref[0]
    kr_even = pltpu.roll(kc, ROPE_LO - MISC_KR, 1) + pltpu.roll(ks, ROPE_LO - MISC_KRR, 1)
    kr_odd = pltpu.roll(kc, LANES - MISC_KR, 1) + pltpu.roll(ks, LANES - MISC_KRR, 1)
    kn = _dot(ckv, wuk_ref[0])
    klane = lax.broadcasted_iota(jnp.int32, (tm, LANES), 1)
    for hd in range(MLA_HEADS):
        pair = kn[:, (hd // 2) * LANES:(hd // 2 + 1) * LANES]
        slot = (jnp.where(klane < MLA_NOPE, pair, kr_even) if hd % 2 == 0
                else jnp.where(klane >= ODD_NOPE_LO, pair, kr_odd))
        k_out[0, :, hd * HEAD_SLOT:(hd + 1) * HEAD_SLOT] = slot.astype(BF16)
    vt = _dot_nt(wuv_ref[0], ckv)
    vrow = lax.broadcasted_iota(jnp.int32, vt.shape, 0)
    v_out[0] = jnp.where(vrow % VT_ROWS == MLA_V, 1.0, vt).astype(BF16)

    u = _gelu_tanh(z[:, Z_SU:Z_SU + D_SG])
    gv = _gelu_tanh(z[:, Z_SV:Z_SV + D_SG])
    mu = jnp.mean(gv, axis=-1, keepdims=True)
    dv = gv - mu
    var = jnp.mean(dv * dv, axis=-1, keepdims=True)
    vn = (dv * lax.rsqrt(var + EPS) * lng_ref[0] + lnb_ref[0]).astype(BF16)
    trow = lax.broadcasted_iota(jnp.int32, (SG_CHUNK, SG_CHUNK), 0)
    tcol = lax.broadcasted_iota(jnp.int32, (SG_CHUNK, SG_CHUNK), 1)
    wmix = [jnp.where(trow >= tcol, ws_ref[0, i], 0.0).astype(BF16) for i in range(SG_HEADS)]
    lane = lax.broadcasted_iota(jnp.int32, (SG_CHUNK, LANES), 1)
    lo_half = lane < SG_HEAD_DIM
    nchunk = tm // SG_CHUNK
    for p in range(D_SG // LANES):
        cols = slice(p * LANES, (p + 1) * LANES)
        vp = jnp.concatenate([vn[c * SG_CHUNK:(c + 1) * SG_CHUNK, cols] for c in range(nchunk)], axis=1)
        m_lo = _dot(wmix[2 * p], vp)
        m_hi = _dot(wmix[2 * p + 1], vp)
        for c in range(nchunk):
            rows = slice(c * SG_CHUNK, (c + 1) * SG_CHUNK)
            blk = slice(c * LANES, (c + 1) * LANES)
            mixed = jnp.where(lo_half, m_lo[:, blk], m_hi[:, blk])
            sg_out[0, rows, cols] = (u[rows, cols] * (mixed + bs_ref[0, :, cols])).astype(BF16)

    gq_out[0] = z[:, Z_GQ:Z_GQ + D_GQK] * (float(GLA_DK) ** -0.5)
    gk_out[0] = z[:, Z_GK:Z_GK + D_GQK]
    gv_out[0] = z[:, Z_GV:Z_GV + D_GLA]
    gate = _dot(misc.astype(BF16), wg2_ref[0]) + bg_ref[0]
    gla_out[0] = _log_sigmoid(gate) * (1.0 / GLA_GATE_TAU)
    gr = z[:, Z_GR:Z_GR + D_GLA]
    gr_out[0] = gr * _sigmoid(gr)


def _pre_call(x, mod, w, tables, l):
    b, s, d = x.shape
    tm = TM_PRE
    tok = lambda n, dt: jax.ShapeDtypeStruct((b, s, n), dt)
    tspec = lambda n: pl.BlockSpec((1, tm, n), lambda i, j: (i, j, 0))
    tspec_t = lambda n: pl.BlockSpec((1, n, tm), lambda i, j: (i, 0, j))
    consts = [w["ng1"], w["win"], w["qg"], w["kvg"], w["wuq"], w["wuqr"], w["wuk"], w["wuv"]]
    consts2 = [w["lng"], w["lnb"], w["ws"], w["bs"], w["wg2"], w["bg"]]
    in_specs = ([tspec(d), _mod_spec(d, l)]
                + [_layer_spec(a.shape, l) for a in consts]
                + [tspec(LANES), tspec(LANES), tspec_t(MLA_ROPE), tspec_t(MLA_ROPE)]
                + [_layer_spec(a.shape, l) for a in consts2])
    outs = [(MLA_HEADS * HEAD_SLOT, BF16), (MLA_HEADS * HEAD_SLOT, BF16), (D_MLA, BF16),
            (D_SG, BF16), (D_GQK, F32), (D_GQK, F32), (D_GQK, F32), (D_GLA, F32), (D_GLA, F32)]
    out_specs = [tspec(n) for n, _ in outs]
    out_shape = [tok(n, dt) for n, dt in outs]
    out_specs[0] = tspec_t(MLA_HEADS * HEAD_SLOT)
    out_shape[0] = jax.ShapeDtypeStruct((b, MLA_HEADS * HEAD_SLOT, s), BF16)
    out_specs[2] = tspec_t(MLA_HEADS * VT_ROWS)
    out_shape[2] = jax.ShapeDtypeStruct((b, MLA_HEADS * VT_ROWS, s), BF16)
    return pl.pallas_call(
        _pre_kernel,
        grid=(b, s // tm),
        in_specs=in_specs,
        out_specs=out_specs,
        out_shape=out_shape,
        compiler_params=pltpu.CompilerParams(
            dimension_semantics=("parallel", "parallel"), vmem_limit_bytes=VMEM_LIMIT),
        name="pre_mix",
    )(x, mod, *consts, *tables, *consts2)


def _attn_kernel(q_ref, k_ref, vt_ref, o_ref, m_s, acc_s, st_s):
    tq = q_ref.shape[2]
    tk = tq
    qi = pl.program_id(1)
    hk = tk // 2
    tri = (lax.broadcasted_iota(jnp.int32, (hk, hk), 0)
           <= lax.broadcasted_iota(jnp.int32, (hk, hk), 1))

    m_s[...] = jnp.full(m_s.shape, NEG_BIG, F32)
    acc_s[...] = jnp.zeros(acc_s.shape, F32)

    def run_heads(scores, softmax_pv):
        for hd in range(ATTN_LOOKAHEAD):
            scores(hd)
        for hd in range(MLA_HEADS):
            if hd + ATTN_LOOKAHEAD < MLA_HEADS:
                scores(hd + ATTN_LOOKAHEAD)
            softmax_pv(hd)

    def full_step(j):
        off = pl.multiple_of(j * tk, tk)

        def scores(hd):
            hcols = slice(hd * HEAD_SLOT, (hd + 1) * HEAD_SLOT)
            st_s[hd] = _dot(k_ref[0, pl.ds(off, tk), hcols], q_ref[0, hcols, :])

        def softmax_pv(hd):
            m_old = m_s[hd]
            m_new = jnp.maximum(m_old, jnp.max(st_s[hd], axis=0, keepdims=True))
            alpha = jnp.exp2(m_old - m_new)
            m_s[hd] = m_new
            e = jnp.exp2((st_s[hd] - m_new).astype(BF16))
            vrows = slice(hd * VT_ROWS, (hd + 1) * VT_ROWS)
            pv = _dot(vt_ref[0, vrows, pl.ds(off, tk)], e)
            acc_s[vrows, :] = alpha * acc_s[vrows, :] + pv

        run_heads(scores, softmax_pv)

    def diag_step():
        off_a = pl.multiple_of(qi * tk, tk)
        off_b = pl.multiple_of(qi * tk + hk, hk)

        def scores(hd):
            hcols = slice(hd * HEAD_SLOT, (hd + 1) * HEAD_SLOT)
            sa = _dot(k_ref[0, pl.ds(off_a, hk), hcols], q_ref[0, hcols, :])
            sb = _dot(k_ref[0, pl.ds(off_b, hk), hcols], q_ref[0, hcols, hk:])
            st_s[hd, 0:hk, 0:hk] = jnp.where(tri, sa[:, 0:hk], NEG_BIG)
            st_s[hd, 0:hk, hk:] = sa[:, hk:]
            st_s[hd, hk:, hk:] = jnp.where(tri, sb, NEG_BIG)

        def softmax_pv(hd):
            m_old = m_s[hd]
            ma = jnp.max(st_s[hd, 0:hk, :], axis=0, keepdims=True)
            mb = jnp.max(st_s[hd, hk:, hk:], axis=0, keepdims=True)
            m_new = jnp.maximum(m_old, jnp.concatenate(
                [ma[:, 0:hk], jnp.maximum(ma[:, hk:], mb)], axis=1))
            alpha = jnp.exp2(m_old - m_new)
            m_s[hd] = m_new
            ea = jnp.exp2((st_s[hd, 0:hk, :] - m_new).astype(BF16))
            eb = jnp.exp2((st_s[hd, hk:, hk:] - m_new[:, hk:]).astype(BF16))
            vrows = slice(hd * VT_ROWS, (hd + 1) * VT_ROWS)
            pva = _dot(vt_ref[0, vrows, pl.ds(off_a, hk)], ea)
            pvb = _dot(vt_ref[0, vrows, pl.ds(off_b, hk)], eb)
            acc_s[vrows, 0:hk] = alpha[:, 0:hk] * acc_s[vrows, 0:hk] + pva[:, 0:hk]
            acc_s[vrows, hk:] = alpha[:, hk:] * acc_s[vrows, hk:] + pva[:, hk:] + pvb

        run_heads(scores, softmax_pv)

    def body(j, carry):
        full_step(j)
        return carry

    lax.fori_loop(0, qi, body, 0)
    diag_step()

    for p in range(MLA_HEADS // 2):
        halves = []
        for hd in (2 * p, 2 * p + 1):
            r0 = hd * VT_ROWS
            halves.append(acc_s[r0:r0 + MLA_V, :] * (1.0 / acc_s[r0 + MLA_V:r0 + MLA_V + 1, :]))
        ot = jnp.concatenate(halves, axis=0)
        o_ref[0, :, p * LANES:(p + 1) * LANES] = ot.T.astype(BF16)


def _attn_call(qt, k, vt):
    b, s, _ = k.shape
    tq = TQ_ATTN
    return pl.pallas_call(
        _attn_kernel,
        grid=(b, s // tq),
        in_specs=[pl.BlockSpec((1, MLA_HEADS * HEAD_SLOT, tq), lambda i, j: (i, 0, j)),
                  pl.BlockSpec((1, s, MLA_HEADS * HEAD_SLOT), lambda i, j: (i, 0, 0)),
                  pl.BlockSpec((1, MLA_HEADS * VT_ROWS, s), lambda i, j: (i, 0, 0))],
        out_specs=pl.BlockSpec((1, tq, D_MLA), lambda i, j: (i, j, 0)),
        out_shape=jax.ShapeDtypeStruct((b, s, D_MLA), BF16),
        scratch_shapes=[pltpu.VMEM((MLA_HEADS, 1, tq), F32),
                        pltpu.VMEM((MLA_HEADS * VT_ROWS, tq), F32),
                        pltpu.VMEM((MLA_HEADS, tq, tq), F32)],
        compiler_params=pltpu.CompilerParams(
            dimension_semantics=("parallel", "arbitrary"), vmem_limit_bytes=VMEM_LIMIT),
        name="mla_attention",
    )(qt, k, vt)


def _gla_kernel(q_ref, k_ref, la_ref, v_ref, r_ref, g_ref, o_ref, st_ref):
    tg = q_ref.shape[1]
    ch = GLA_CHUNK
    sb = GLA_SUB

    @pl.when(pl.program_id(1) == 0)
    def _():
        st_ref[...] = jnp.zeros_like(st_ref)

    q = q_ref[0]
    k = k_ref[0]
    v = v_ref[0]
    ti = lax.broadcasted_iota(jnp.int32, (ch, ch), 0)
    tj = lax.broadcasted_iota(jnp.int32, (ch, ch), 1)
    tri = jnp.where(tj <= ti, 1.0, 0.0).astype(BF16)
    la = la_ref[0] * LOG2_E
    la_hi = la.astype(BF16)
    la_r1 = la - la_hi.astype(F32)
    la_mid = la_r1.astype(BF16)
    la_lo = (la_r1 - la_mid.astype(F32)).astype(BF16)
    b = jnp.concatenate(
        [_dot(tri, la_hi[c * ch:(c + 1) * ch]) + _dot(tri, la_mid[c * ch:(c + 1) * ch])
         + _dot(tri, la_lo[c * ch:(c + 1) * ch]) for c in range(tg // ch)], axis=0)

    er = lax.broadcasted_iota(jnp.int32, (D_GQK, D_GLA), 0)
    ec = lax.broadcasted_iota(jnp.int32, (D_GQK, D_GLA), 1)
    head_sum = jnp.where((er // GLA_DK) == (ec // GLA_DV), 1.0, 0.0).astype(BF16)

    rowmod = lax.broadcasted_iota(jnp.int32, (tg, D_GQK), 0) % sb
    acc = _dot((q * k).astype(BF16), head_sum) * v
    assert sb == SUBLANE_ROWS
    in_tile_shift = lambda a, s: pltpu.roll(a.reshape(tg // sb, sb, a.shape[1]), s, 1).reshape(a.shape)
    for s in range(1, sb):
        ks = in_tile_shift(k, s)
        bs = in_tile_shift(b, s)
        vs = in_tile_shift(v, s)
        p = jnp.where(rowmod >= s, q * ks * jnp.exp2(b - bs), 0.0)
        acc = acc + _dot(p.astype(BF16), head_sum) * vs

    sr = lax.broadcasted_iota(jnp.int32, (D_GLA, D_GQK), 0)
    sc = lax.broadcasted_iota(jnp.int32, (D_GLA, D_GQK), 1)
    same_head = (sr // GLA_DV) == (sc // GLA_DK)
    gr = lax.broadcasted_iota(jnp.int32, (D_GLA, D_GLA), 0)
    gc = lax.broadcasted_iota(jnp.int32, (D_GLA, D_GLA), 1)
    head_mean = jnp.where((gr // GLA_DV) == (gc // GLA_DV), 1.0 / GLA_DV, 0.0).astype(BF16)
    klane = lax.broadcasted_iota(jnp.int32, (sb, D_GQK), 1) // GLA_DK
    vlane = lax.broadcasted_iota(jnp.int32, (sb, D_GLA), 1) // GLA_DV
    krow = lax.broadcasted_iota(jnp.int32, (ch, D_GQK), 0)
    nch = tg // ch
    chunk = lambda a, c: a[c * ch:(c + 1) * ch]
    vb = v.astype(BF16)

    scores = []
    for c in range(nch):
        bc, qc, kc = chunk(b, c), chunk(q, c), chunk(k, c)
        for blk in range(1, ch // sb):
            lo_r = blk * sb
            r = bc[lo_r:lo_r + 1, :]
            qt = qc[lo_r:lo_r + sb] * jnp.exp2(bc[lo_r:lo_r + sb] - r)
            kt = (kc * jnp.exp2(jnp.where(krow < lo_r, r - bc, NEG_BIG))).astype(BF16)
            qexp = jnp.concatenate([jnp.where(klane == hd, qt, 0.0) for hd in range(GLA_HEADS)], axis=0)
            scores.append(_dot_nt(qexp.astype(BF16), kt))
    upds = []
    for c in range(nch):
        bc = chunk(b, c)
        kd = (chunk(k, c) * jnp.exp2(bc[ch - 1:ch, :] - bc)).astype(BF16)
        upds.append(jnp.where(same_head, _dot_tn(chunk(vb, c), kd), 0.0))
    o_sub = []
    for c in range(nch):
        pieces = [jnp.zeros((sb, D_GLA), F32)]
        for blk in range(1, ch // sb):
            oh = _dot(scores[c * (ch // sb - 1) + blk - 1].astype(BF16), chunk(vb, c))
            piece = jnp.where(vlane == 0, oh[0:sb], 0.0)
            for hd in range(1, GLA_HEADS):
                piece = piece + jnp.where(vlane == hd, oh[hd * sb:(hd + 1) * sb], 0.0)
            pieces.append(piece)
        o_sub.append(jnp.concatenate(pieces, axis=0))
    states = []
    st = st_ref[...]
    for c in range(nch):
        states.append(st.astype(BF16))
        st = st * jnp.exp2(chunk(b, c)[ch - 1:ch, :]) + upds[c]
    st_ref[...] = st
    o = jnp.concatenate(
        [chunk(acc, c) + o_sub[c]
         + _dot_nt((chunk(q, c) * jnp.exp2(chunk(b, c))).astype(BF16), states[c]) for c in range(nch)], axis=0)
    o2 = o * o
    hi = o2.astype(BF16)
    lo = (o2 - hi.astype(F32)).astype(BF16)
    ms = _dot(hi, head_mean) + _dot(lo, head_mean)
    o_ref[0] = (o * lax.rsqrt(ms + EPS) * g_ref[0] * r_ref[0]).astype(BF16)


def _gla_call(gq, gk, gla, gv, gr, g256, l):
    b, s, _ = gq.shape
    tg = TG_GLA
    tspec = lambda n: pl.BlockSpec((1, tg, n), lambda i, j: (i, j, 0))
    return pl.pallas_call(
        _gla_kernel,
        grid=(b, s // tg),
        in_specs=[tspec(D_GQK), tspec(D_GQK), tspec(D_GQK), tspec(D_GLA), tspec(D_GLA),
                  _layer_spec(g256.shape, l)],
        out_specs=tspec(D_GLA),
        out_shape=jax.ShapeDtypeStruct((b, s, D_GLA), BF16),
        scratch_shapes=[pltpu.VMEM((D_GLA, D_GQK), F32)],
        compiler_params=pltpu.CompilerParams(
            dimension_semantics=("parallel", "arbitrary"), vmem_limit_bytes=VMEM_LIMIT),
        name="gla",
    )(gq, gk, gla, gv, gr, g256)


def _post_kernel(x_ref, oa_ref, ob_ref, oc_ref, mod_ref, wo_ref, ng_ref, wup_ref, cw_ref, cb_ref,
                 wdn_ref, fg_ref, o_ref, zbuf, act, *, final):
    tm = x_ref.shape[1]
    halo = 8
    g1 = mod_ref[0, 0, 2:3, :]
    sh2 = mod_ref[0, 0, 3:4, :]
    sc2 = mod_ref[0, 0, 4:5, :]
    g2 = mod_ref[0, 0, 5:6, :]
    mix = (_dot(oa_ref[0], wo_ref[0, 0:D_MLA, :])
           + _dot(ob_ref[0], wo_ref[0, D_MLA:D_MLA + D_SG, :])
           + _dot(oc_ref[0], wo_ref[0, D_MLA + D_SG:, :]))
    x1 = x_ref[0] + g1 * mix
    h2 = _rms(x1, ng_ref[0]) * (1.0 + sc2) + sh2

    @pl.when(pl.program_id(1) == 0)
    def _():
        zbuf[0:halo, :] = jnp.zeros((halo, 2 * D_FF), F32)

    zbuf[halo:halo + tm, :] = _dot(h2.astype(BF16), wup_ref[0])
    cc = FFN_CHUNK
    for j in range(D_FF // cc):
        parts = []
        for cols in (slice(j * cc, (j + 1) * cc), slice(D_FF + j * cc, D_FF + (j + 1) * cc)):
            zc = cb_ref[0, :, cols] + cw_ref[0, 2:3, cols] * zbuf[halo:halo + tm, cols]
            zc = zc + cw_ref[0, 0:1, cols] * zbuf[halo - 2:halo - 2 + tm, cols]
            zc = zc + cw_ref[0, 1:2, cols] * zbuf[halo - 1:halo - 1 + tm, cols]
            parts.append(zc)
        val, gate = parts
        act[:, j * cc:(j + 1) * cc] = (gate * _sigmoid(gate) * val).astype(BF16)
    zbuf[halo - 2:halo, :] = zbuf[halo + tm - 2:halo + tm, :]
    x2 = x1 + g2 * _dot(act[...], wdn_ref[0])
    if final:
        x2 = _rms(x2, fg_ref[...])
    o_ref[0] = x2


def _post_call(x, oa, ob, oc, mod, w, final_g, l, final):
    b, s, d = x.shape
    tm = TM_POST
    tspec = lambda n: pl.BlockSpec((1, tm, n), lambda i, j: (i, j, 0))
    consts = [w["wo"], w["ng2"], w["wup"], w["cw"], w["cb"], w["wdn"]]
    return pl.pallas_call(
        functools.partial(_post_kernel, final=final),
        grid=(b, s // tm),
        in_specs=[tspec(d), tspec(D_MLA), tspec(D_SG), tspec(D_GLA), _mod_spec(d, l)]
                 + [_layer_spec(a.shape, l) for a in consts] + [_const_spec(final_g.shape)],
        out_specs=tspec(d),
        out_shape=jax.ShapeDtypeStruct((b, s, d), F32),
        scratch_shapes=[pltpu.VMEM((8 + tm, 2 * D_FF), F32), pltpu.VMEM((tm, D_FF), BF16)],
        compiler_params=pltpu.CompilerParams(
            dimension_semantics=("parallel", "arbitrary"), vmem_limit_bytes=VMEM_LIMIT),
        name="post_ffn",
    )(x, oa, ob, oc, mod, *consts, final_g)


def _rot_half_cols(w):
    half = MLA_ROPE // 2
    return jnp.concatenate([-w[..., half:], w[..., :half]], axis=-1)


def _stacked_weights(p):
    depth, d, _ = p["w_in"].shape
    splits = (MLA_Q_LORA, MLA_KV_LORA, MLA_ROPE, D_SG, D_SG, D_GQK, D_GQK, D_GLA, GLA_GATE_RANK, D_GLA)
    offs = [0]
    for n in splits:
        offs.append(offs[-1] + n)
    w_in = p["w_in"]
    col = lambda i: w_in[..., offs[i]:offs[i + 1]]
    win = jnp.concatenate([
        w_in[..., :offs[2]],
        col(8), col(2), _rot_half_cols(col(2)), jnp.zeros((depth, d, LANES - MISC_KRR - MLA_ROPE), F32),
        w_in[..., offs[3]:offs[8]], col(9)], axis=-1).astype(BF16)
    assert win.shape[-1] == Z_COLS

    keep3 = ((0, 0), (0, 0), (0, 0))
    wq = p["mla_w_uq"].astype(BF16).reshape(depth, MLA_Q_LORA, MLA_HEADS, MLA_NOPE + MLA_ROPE)
    nope, rope = wq[..., :MLA_NOPE], wq[..., MLA_NOPE:]
    pad_cols = jnp.zeros(rope.shape[:-1] + (HEAD_SLOT - MLA_NOPE - MLA_ROPE,), BF16)
    odd_head = (jnp.arange(MLA_HEADS) % 2 == 1)[None, None, :, None]
    wuq = jnp.where(odd_head, jnp.concatenate([rope, pad_cols, nope], axis=-1),
                    jnp.concatenate([nope, rope, pad_cols], axis=-1))
    wuq = jnp.swapaxes(wuq.reshape(depth, MLA_Q_LORA, MLA_HEADS * HEAD_SLOT), 1, 2)
    wuqr = _rot_half_cols(rope).reshape(depth, MLA_Q_LORA, MLA_HEADS * MLA_ROPE)
    wuqr = jnp.swapaxes(wuqr, 1, 2)
    wkv = p["mla_w_ukv"].astype(BF16).reshape(depth, MLA_KV_LORA, MLA_HEADS, MLA_NOPE + MLA_V)
    wuk = wkv[..., :MLA_NOPE].reshape(depth, MLA_KV_LORA, MLA_HEADS * MLA_NOPE)
    wuv = jnp.pad(wkv[..., MLA_NOPE:], keep3 + ((0, VT_ROWS - MLA_V),))
    wuv = jnp.swapaxes(wuv.reshape(depth, MLA_KV_LORA, MLA_HEADS * VT_ROWS), 1, 2)

    bs = jnp.repeat(jnp.swapaxes(p["sg_b_s"], 1, 2), SG_HEAD_DIM, axis=2)
    wg2 = jnp.pad(p["gla_w_gate2"].astype(BF16), ((0, 0), (0, LANES - GLA_GATE_RANK), (0, 0)))
    row = lambda a: a.reshape(depth, 1, -1)
    return {
        "ng1": row(p["norm_mix_g"]), "win": win,
        "qg": row(p["mla_q_norm_g"]), "kvg": row(p["mla_kv_norm_g"]),
        "wuq": wuq, "wuqr": wuqr, "wuk": wuk, "wuv": wuv,
        "lng": row(p["sg_ln_g"]), "lnb": row(p["sg_ln_b"]), "ws": p["sg_w_s"], "bs": bs,
        "wg2": wg2, "bg": row(p["gla_b_gate"]),
        "gng": row(jnp.tile(p["gla_norm_g"], (1, GLA_HEADS))),
        "wo": p["w_out"].astype(BF16), "ng2": row(p["norm_ffn_g"]),
        "wup": p["ffn_w_up"].astype(BF16), "cw": p["ffn_conv_w"],
        "cb": row(p["ffn_conv_b"]), "wdn": p["ffn_w_down"].astype(BF16),
    }


def kernel(x, c, positions, mod_w, mod_b, norm_mix_g, norm_ffn_g, w_in, mla_q_norm_g, mla_kv_norm_g, mla_w_uq, mla_w_ukv, sg_ln_g, sg_ln_b, sg_w_s, sg_b_s, gla_w_gate2, gla_b_gate, gla_norm_g, w_out, ffn_w_up, ffn_conv_w, ffn_conv_b, ffn_w_down, final_norm_g):
    p = dict(norm_mix_g=norm_mix_g, norm_ffn_g=norm_ffn_g, w_in=w_in, mla_q_norm_g=mla_q_norm_g,
             mla_kv_norm_g=mla_kv_norm_g, mla_w_uq=mla_w_uq, mla_w_ukv=mla_w_ukv, sg_ln_g=sg_ln_g,
             sg_ln_b=sg_ln_b, sg_w_s=sg_w_s, sg_b_s=sg_b_s, gla_w_gate2=gla_w_gate2,
             gla_b_gate=gla_b_gate, gla_norm_g=gla_norm_g, w_out=w_out, ffn_w_up=ffn_w_up,
             ffn_conv_w=ffn_conv_w, ffn_conv_b=ffn_conv_b, ffn_w_down=ffn_w_down)
    depth = mod_w.shape[0]
    b = x.shape[0]
    mod = _modulation(c, mod_w, mod_b).reshape(depth, b, 6, D_MODEL)
    tables = _rope_tables(positions)
    fg = final_norm_g.reshape(1, -1)
    w = _stacked_weights(p)
    for l in range(depth):
        q, k, v, o_sg, gq, gk, gla, gv, gr = _pre_call(x, mod, w, tables, l)
        o_mla = _attn_call(q, k, v)
        o_gla = _gla_call(gq, gk, gla, gv, gr, w["gng"], l)
        x = _post_call(x, o_mla, o_sg, o_gla, mod, w, fg, l, final=(l == depth - 1))
    return x
```
